```python
import jax, jax.numpy as jnp
from jax import lax
import numpy as np

D_MODEL = 1024
BATCH = 2
SEQ = 8192
DEPTH = 4

EPS = 1e-6
D_LRU = 384
LRU_HEADS = 6
LRU_HEAD_DIM = D_LRU // LRU_HEADS
CONV_WIDTH = 4
LRU_C = 8.0
MLA_HEADS = 6
QK_NOPE_DIM = 64
QK_ROPE_DIM = 32
V_HEAD_DIM = 64
D_MLA = MLA_HEADS * V_HEAD_DIM
Q_LORA_RANK = 384
KV_LORA_RANK = 256
ROPE_BASE = 10000.0
Q_BLOCK = 128
POOL_WINDOWS = (2, 4, 8, 16)
POOL_GROUP_DIM = 64
D_POOL = POOL_GROUP_DIM * len(POOL_WINDOWS)
D_MIX = D_LRU + D_MLA + D_POOL
IN_SIZES = (D_LRU, D_LRU, Q_LORA_RANK, KV_LORA_RANK, QK_ROPE_DIM, D_MLA, D_POOL, D_POOL)
D_IN = sum(IN_SIZES)

kernel_name = "hybrid_lru_mla_pool_trunk"


def rms_norm(x, g):
    xf = x.astype(jnp.float32)
    y = xf * lax.rsqrt(jnp.mean(xf * xf, axis=-1, keepdims=True) + EPS)
    return (y * g.astype(jnp.float32)).astype(x.dtype)


def rope_tables(seq_len):
    pos = jnp.arange(seq_len, dtype=jnp.float32)
    inv_freq = ROPE_BASE ** (-jnp.arange(0, QK_ROPE_DIM, 2, dtype=jnp.float32) / QK_ROPE_DIM)
    ang = pos[:, None] * inv_freq[None, :]
    return jnp.cos(ang), jnp.sin(ang)


def apply_rope(x, cos, sin):
    shape = (1, x.shape[1]) + (1,) * (x.ndim - 3) + (QK_ROPE_DIM // 2,)
    c = cos.reshape(shape)
    s = sin.reshape(shape)
    xf = x.astype(jnp.float32)
    x1, x2 = jnp.split(xf, 2, axis=-1)
    return jnp.concatenate([x1 * c - x2 * s, x1 * s + x2 * c], axis=-1).astype(x.dtype)


def causal_depthwise_conv(x, w, b):
    S = x.shape[1]
    xp = jnp.pad(x, ((0, 0), (CONV_WIDTH - 1, 0), (0, 0)))
    y = b
    for k in range(CONV_WIDTH):
        y = y + xp[:, k:k + S, :] * w[k]
    return y


def rg_lru(x, w_r, b_r, w_i, b_i, lam):
    B, S, _ = x.shape
    xf = x.astype(jnp.float32)
    xh = xf.reshape(B, S, LRU_HEADS, LRU_HEAD_DIM)
    r = jax.nn.sigmoid(jnp.einsum('bshi,hij->bshj', xh, w_r.astype(jnp.float32)).reshape(B, S, D_LRU) + b_r.astype(jnp.float32))
    i = jax.nn.sigmoid(jnp.einsum('bshi,hij->bshj', xh, w_i.astype(jnp.float32)).reshape(B, S, D_LRU) + b_i.astype(jnp.float32))
    log_a = -LRU_C * r * jax.nn.softplus(-lam.astype(jnp.float32))
    a = jnp.exp(log_a)
    u = jnp.sqrt(-jnp.expm1(2.0 * log_a)) * (i * xf)

    def combine(left, right):
        a_l, h_l = left
        a_r, h_r = right
        return a_l * a_r, a_r * h_l + h_r

    _, h = lax.associative_scan(combine, (a, u), axis=1)
    return h.astype(x.dtype)


def mla(c_q, c_kv, k_r, q_norm_g, w_uq, kv_norm_g, w_ukv, cos, sin):
    B, S, _ = c_q.shape
    q = (rms_norm(c_q, q_norm_g) @ w_uq).reshape(B, S, MLA_HEADS, QK_NOPE_DIM + QK_ROPE_DIM)
    q_nope = q[..., :QK_NOPE_DIM]
    q_rope = apply_rope(q[..., QK_NOPE_DIM:], cos, sin)
    kv = (rms_norm(c_kv, kv_norm_g) @ w_ukv).reshape(B, S, MLA_HEADS, QK_NOPE_DIM + V_HEAD_DIM)
    k_nope = kv[..., :QK_NOPE_DIM]
    v = kv[..., QK_NOPE_DIM:]
    k_rope = apply_rope(k_r, cos, sin)
    scale = (QK_NOPE_DIM + QK_ROPE_DIM) ** -0.5
    outs = []
    for blk in range(S // Q_BLOCK):
        q0 = blk * Q_BLOCK
        q1 = q0 + Q_BLOCK
        s = (jnp.einsum('bqhd,bkhd->bhqk', q_nope[:, q0:q1], k_nope[:, :q1])
             + jnp.einsum('bqhr,bkr->bhqk', q_rope[:, q0:q1], k_rope[:, :q1]))
        s = s.astype(jnp.float32) * scale
        mask = jnp.arange(q1)[None, :] <= (q0 + jnp.arange(Q_BLOCK))[:, None]
        s = jnp.where(mask, s, -1e30)
        p = jax.nn.softmax(s, axis=-1).astype(v.dtype)
        outs.append(jnp.einsum('bhqk,bkhd->bqhd', p, v[:, :q1]))
    return jnp.concatenate(outs, axis=1).reshape(B, S, D_MLA)


def multi_scale_pool(x, w_pool, pool_scale):
    B, S, _ = x.shape
    xf = x.astype(jnp.float32)
    n_seen = jnp.arange(1, S + 1, dtype=jnp.float32)
    outs = []
    for g, w in enumerate(POOL_WINDOWS):
        xg = xf[..., g * POOL_GROUP_DIM:(g + 1) * POOL_GROUP_DIM]
        cs = jnp.cumsum(xg, axis=1)
        lower = jnp.pad(cs, ((0, 0), (w, 0), (0, 0)))[:, :S]
        cnt = jnp.minimum(n_seen, float(w))[None, :, None]
        outs.append((cs - lower) / cnt - xg)
    pooled = jnp.stack(outs, axis=2)
    y = jnp.einsum('bsgi,gij->bsgj', pooled, w_pool.astype(jnp.float32)).reshape(B, S, D_POOL)
    return (y * pool_scale.astype(jnp.float32)).astype(x.dtype)


def setup_inputs(seed: int = 0) -> dict:
    key = jax.random.key(seed)
    ks = jax.random.split(key, 20)

    def nrm(k, shape, scale):
        return scale * jax.random.normal(k, shape, jnp.float32)

    u = jax.random.uniform(ks[9], (DEPTH, D_LRU), jnp.float32, minval=0.9, maxval=0.999)
    s = u ** (1.0 / LRU_C)
    lru_lambda = jnp.log(s) - jnp.log1p(-s)
    return {
        "x": nrm(ks[0], (BATCH, SEQ, D_MODEL), 1.0),
        "norm_g": 1.0 + nrm(ks[1], (DEPTH, D_MODEL), 0.02),
        "w_in": nrm(ks[2], (DEPTH, D_MODEL, D_IN), D_MODEL ** -0.5),
        "conv_w": nrm(ks[3], (DEPTH, CONV_WIDTH, D_LRU), CONV_WIDTH ** -0.5),
        "conv_b": nrm(ks[4], (DEPTH, D_LRU), 0.01),
        "w_rg": nrm(ks[5], (DEPTH, LRU_HEADS, LRU_HEAD_DIM, LRU_HEAD_DIM), LRU_HEAD_DIM ** -0.5),
        "b_rg": nrm(ks[6], (DEPTH, D_LRU), 0.01),
        "w_ig": nrm(ks[7], (DEPTH, LRU_HEADS, LRU_HEAD_DIM, LRU_HEAD_DIM), LRU_HEAD_DIM ** -0.5),
        "b_ig": nrm(ks[8], (DEPTH, D_LRU), 0.01),
        "lru_lambda": lru_lambda,
        "q_norm_g": 1.0 + nrm(ks[10], (DEPTH, Q_LORA_RANK), 0.02),
        "w_uq": nrm(ks[11], (DEPTH, Q_LORA_RANK, MLA_HEADS * (QK_NOPE_DIM + QK_ROPE_DIM)), Q_LORA_RANK ** -0.5),
        "kv_norm_g": 1.0 + nrm(ks[12], (DEPTH, KV_LORA_RANK), 0.02),
        "w_ukv": nrm(ks[13], (DEPTH, KV_LORA_RANK, MLA_HEADS * (QK_NOPE_DIM + V_HEAD_DIM)), KV_LORA_RANK ** -0.5),
        "w_pool": nrm(ks[14], (DEPTH, len(POOL_WINDOWS), POOL_GROUP_DIM, POOL_GROUP_DIM), POOL_GROUP_DIM ** -0.5),
        "pool_scale": 1.0 + nrm(ks[15], (DEPTH, D_POOL), 0.1),
        "w_out": nrm(ks[16], (DEPTH, D_MIX, D_MODEL), D_MIX ** -0.5),
        "final_norm_g": 1.0 + nrm(ks[17], (D_MODEL,), 0.02),
    }


def reference(x, norm_g, w_in, conv_w, conv_b, w_rg, b_rg, w_ig, b_ig, lru_lambda,
              q_norm_g, w_uq, kv_norm_g, w_ukv, w_pool, pool_scale, w_out, final_norm_g):
    S = x.shape[1]
    cos, sin = rope_tables(S)
    offsets = np.cumsum(IN_SIZES)[:-1].tolist()
    for l in range(DEPTH):
        h = rms_norm(x, norm_g[l])
        z = h @ w_in[l]
        za, ga, cq, ckv, kr, gb, zc, gc = jnp.split(z, offsets, axis=-1)
        xa = causal_depthwise_conv(za, conv_w[l], conv_b[l])
        ya = rg_lru(xa, w_rg[l], b_rg[l], w_ig[l], b_ig[l], lru_lambda[l]) * jax.nn.silu(ga)
        yb = mla(cq, ckv, kr, q_norm_g[l], w_uq[l], kv_norm_g[l], w_ukv[l], cos, sin) * jax.nn.silu(gb)
        yc = multi_scale_pool(zc, w_pool[l], pool_scale[l]) * jax.nn.silu(gc)
        y = jnp.concatenate([ya, yb, yc], axis=-1)
        x = x + y @ w_out[l]
    return rms_norm(x, final_norm_g)
```

```python
import functools

import jax
import jax.numpy as jnp
import numpy as np
from jax import lax
from jax.experimental import pallas as pl
from jax.experimental.pallas import tpu as pltpu

F32 = jnp.float32
BF16 = jnp.bfloat16

D_MODEL = 1024
DEPTH = 4
EPS = 1e-6
D_LRU = 384
LRU_HEADS = 6
LRU_HEAD_DIM = 64
CONV_WIDTH = 4
LRU_C = 8.0
MLA_HEADS = 6
QK_NOPE_DIM = 64
QK_ROPE_DIM = 32
V_HEAD_DIM = 64
D_MLA = MLA_HEADS * V_HEAD_DIM
Q_LORA_RANK = 384
KV_LORA_RANK = 256
ROPE_BASE = 10000.0
POOL_WINDOWS = (2, 4, 8, 16)
POOL_GROUP_DIM = 64
D_POOL = POOL_GROUP_DIM * len(POOL_WINDOWS)
D_MIX = D_LRU + D_MLA + D_POOL
IN_SIZES = (D_LRU, D_LRU, Q_LORA_RANK, KV_LORA_RANK, QK_ROPE_DIM, D_MLA, D_POOL, D_POOL)

LANES = 128
HEAD_PAD = LANES
D_QK = MLA_HEADS * HEAD_PAD
ROPE_LO = QK_NOPE_DIM
HALF = QK_ROPE_DIM // 2
CONV_HALO = 8
POOL_HALO = 16
NEG = -1e30

C_ZA = 0
C_GA = C_ZA + D_LRU
C_CQ = C_GA + D_LRU
C_CKV = C_CQ + Q_LORA_RANK
C_GB = C_CKV + KV_LORA_RANK
C_ZC = C_GB + D_MLA
C_GC = C_ZC + D_POOL
C_KR = C_GC + D_POOL
D_IN_PAD = C_KR + HEAD_PAD

T_PROJ = 512
T_ATT = 512
T_OUT = 512


def _dot(a, b):
    return jnp.dot(a, b, preferred_element_type=F32)


def _dot_nt(a, b):
    return lax.dot_general(a, b, (((1,), (1,)), ((), ())), preferred_element_type=F32)


def _rms(x, g):
    return x * lax.rsqrt(jnp.mean(x * x, axis=-1, keepdims=True) + EPS) * g


def _silu(x):
    return x * jax.nn.sigmoid(x)


def _rope(x, cos, sin_hi, sin_lo):
    return (x * cos + pltpu.roll(x, LANES - HALF, axis=1) * sin_hi
            + pltpu.roll(x, HALF, axis=1) * sin_lo)


def _proj_kernel(x_ref, ng_ref, win_ref, convw_ref, convb_ref, wg_ref, bg_ref, lam_ref,
                 qg_ref, wuq_ref, kvg_ref, wuk_ref, wvt_ref, wpool_ref, pscale_ref,
                 cos_ref, shi_ref, slo_ref,
                 ya_ref, gbs_ref, yc_ref, q_ref, k_ref, vt_ref,
                 za_ext, zc_ext, h_carry):
    ts = x_ref.shape[0]
    t = pl.program_id(1)

    @pl.when(t == 0)
    def _():
        za_ext[0:CONV_HALO, :] = jnp.zeros((CONV_HALO, D_LRU), F32)
        zc_ext[0:POOL_HALO, :] = jnp.zeros((POOL_HALO, D_POOL), F32)
        h_carry[...] = jnp.zeros_like(h_carry)

    hb = _rms(x_ref[...], ng_ref[...]).astype(BF16)

    zag = _dot(hb, win_ref[:, C_ZA:C_CQ])
    za = zag[:, :D_LRU]
    ga = zag[:, D_LRU:]
    za_ext[CONV_HALO:CONV_HALO + ts, :] = za
    xa = convb_ref[...]
    for k in range(CONV_WIDTH):
        off = CONV_HALO - (CONV_WIDTH - 1) + k
        xa = xa + za_ext[off:off + ts, :] * convw_ref[k:k + 1, :]
    za_ext[0:CONV_HALO, :] = za_ext[ts:ts + CONV_HALO, :]

    gates = jax.nn.sigmoid(_dot(xa.astype(BF16), wg_ref[...]) + bg_ref[...])
    r = gates[:, :D_LRU]
    i = gates[:, D_LRU:]
    nl = -lam_ref[...]
    softplus = jnp.maximum(nl, 0.0) + jnp.log1p(jnp.exp(-jnp.abs(nl)))
    log_a = (-LRU_C) * r * softplus
    a = jnp.exp(log_a)
    h = jnp.sqrt(-jnp.tanh(log_a) * (a * a + 1.0)) * (i * xa)
    row = lax.broadcasted_iota(jnp.int32, (ts, 1), 0)
    d = 1
    while d < ts:
        keep = row >= d
        a_sh = jnp.where(keep, pltpu.roll(a, d, axis=0), 1.0)
        h_sh = jnp.where(keep, pltpu.roll(h, d, axis=0), 0.0)
        h = a * h_sh + h
        a = a * a_sh
        d *= 2
    h = h + a * h_carry[...]
    h_carry[...] = h[ts - 1:ts, :]
    ya_ref[...] = (h * _silu(ga)).astype(BF16)

    cq = _dot(hb, win_ref[:, C_CQ:C_CKV])
    qf = _dot(_rms(cq, qg_ref[...]).astype(BF16), wuq_ref[...])
    ckv = _dot(hb, win_ref[:, C_CKV:C_GB])
    kvn = _rms(ckv, kvg_ref[...]).astype(BF16)
    kn = _dot(kvn, wuk_ref[...])
    vt = _dot_nt(wvt_ref[...], kvn)
    cos = cos_ref[...]
    shi = shi_ref[...]
    slo = slo_ref[...]
    kr = _rope(_dot(hb, win_ref[:, C_KR:D_IN_PAD]), cos, shi, slo)
    scale = (QK_NOPE_DIM + QK_ROPE_DIM) ** -0.5
    for hd in range(MLA_HEADS):
        sl = slice(hd * HEAD_PAD, (hd + 1) * HEAD_PAD)
        q_ref[:, sl] = (_rope(qf[:, sl], cos, shi, slo) * scale).astype(BF16)
        k_ref[:, sl] = (kn[:, sl] + kr).astype(BF16)
    tk = vt_ref.shape[-1]
    for hd in range(MLA_HEADS):
        for c in range(ts // tk):
            vt_ref[hd, c] = vt[hd * V_HEAD_DIM:(hd + 1) * V_HEAD_DIM, c * tk:(c + 1) * tk].astype(BF16)
    gbs_ref[...] = _silu(_dot(hb, win_ref[:, C_GB:C_ZC]))

    zcg = _dot(hb, win_ref[:, C_ZC:C_KR])
    zc = zcg[:, :D_POOL]
    gc = zcg[:, D_POOL:]
    zc_ext[POOL_HALO:POOL_HALO + ts, :] = zc
    lane = lax.broadcasted_iota(jnp.int32, (1, LANES), 1)
    pos1 = (t * ts + row + 1).astype(F32)
    pooled = []
    for blk in range(D_POOL // LANES):
        w_small, w_big = POOL_WINDOWS[2 * blk], POOL_WINDOWS[2 * blk + 1]
        cols = slice(blk * LANES, (blk + 1) * LANES)
        acc = zc[:, cols]
        sums = {}
        for j in range(1, w_big):
            acc = acc + zc_ext[POOL_HALO - j:POOL_HALO - j + ts, cols]
            if j + 1 in (w_small, w_big):
                sums[j + 1] = acc
        first = lane < POOL_GROUP_DIM
        wsum = jnp.where(first, sums[w_small], sums[w_big])
        cnt = jnp.minimum(pos1, jnp.where(first, float(w_small), float(w_big)))
        pooled.append(wsum / cnt - zc[:, cols])
    zc_ext[0:POOL_HALO, :] = zc_ext[ts:ts + POOL_HALO, :]
    pooled = jnp.concatenate(pooled, axis=1).astype(BF16)
    yc_ref[...] = (_dot(pooled, wpool_ref[...]) * pscale_ref[...] * _silu(gc)).astype(BF16)


def _proj_call(x, lw, tabs, t_att):
    B, S, _ = x.shape
    ts = T_PROJ
    nt = S // ts
    row_blk = lambda w: pl.BlockSpec((None, ts, w), lambda b, t: (b, t, 0))
    full = lambda a: pl.BlockSpec(a.shape, lambda b, t: (0,) * a.ndim)
    tab = pl.BlockSpec((ts, LANES), lambda b, t: (t, 0))
    weights = [lw[n] for n in ("norm_g", "w_in", "conv_w", "conv_b", "w_gate", "b_gate", "lam",
                               "q_norm_g", "w_uq", "kv_norm_g", "w_uk", "w_vt", "w_pool", "pool_scale")]
    out_shape = (
        jax.ShapeDtypeStruct((B, S, D_LRU), BF16),
        jax.ShapeDtypeStruct((B, S, D_MLA), F32),
        jax.ShapeDtypeStruct((B, S, D_POOL), BF16),
        jax.ShapeDtypeStruct((B, S, D_QK), BF16),
        jax.ShapeDtypeStruct((B, S, D_QK), BF16),
        jax.ShapeDtypeStruct((B, MLA_HEADS, S // t_att, V_HEAD_DIM, t_att), BF16),
    )
    out_specs = (
        row_blk(D_LRU), row_blk(D_MLA), row_blk(D_POOL), row_blk(D_QK), row_blk(D_QK),
        pl.BlockSpec((None, MLA_HEADS, ts // t_att, V_HEAD_DIM, t_att), lambda b, t: (b, 0, t, 0, 0)),
    )
    return pl.pallas_call(
        _proj_kernel,
        grid=(B, nt),
        in_specs=[row_blk(D_MODEL)] + [full(w) for w in weights] + [tab, tab, tab],
        out_specs=out_specs,
        out_shape=out_shape,
        scratch_shapes=[
            pltpu.VMEM((ts + CONV_HALO, D_LRU), F32),
            pltpu.VMEM((ts + POOL_HALO, D_POOL), F32),
            pltpu.VMEM((1, D_LRU), F32),
        ],
        compiler_params=pltpu.CompilerParams(dimension_semantics=("arbitrary", "arbitrary")),
    )(x, *weights, *tabs)


def _attn_kernel(q_ref, k_ref, vt_ref, g_ref, o_ref):
    T = q_ref.shape[0]
    i = pl.program_id(2)
    qs = [q_ref[:, h * HEAD_PAD:(h + 1) * HEAD_PAD] for h in range(2)]

    def tile(j, carry, masked):
        new = []
        for h in range(2):
            m, l, acc = carry[h]
            start = pl.multiple_of(j * T, T)
            k_t = k_ref[pl.ds(start, T), h * HEAD_PAD:(h + 1) * HEAD_PAD]
            s = _dot_nt(k_t, qs[h])
            if masked:
                kv_pos = lax.broadcasted_iota(jnp.int32, (T, T), 0)
                q_pos = lax.broadcasted_iota(jnp.int32, (T, T), 1)
                s = jnp.where(kv_pos <= q_pos, s, NEG)
            m_new = jnp.maximum(m, jnp.max(s, axis=0, keepdims=True))
            alpha = jnp.exp(m - m_new)
            p = jnp.exp(s - m_new)
            l = alpha * l + jnp.sum(p, axis=0, keepdims=True)
            acc = alpha * acc + _dot(vt_ref[h, j], p.astype(BF16))
            new.append((m_new, l, acc))
        return tuple(new)

    init = tuple((jnp.full((1, T), NEG, F32), jnp.zeros((1, T), F32), jnp.zeros((V_HEAD_DIM, T), F32))
                 for _ in range(2))
    carry = lax.fori_loop(0, i, lambda j, c: tile(j, c, False), init)
    carry = tile(i, carry, True)
    out_t = jnp.concatenate([acc * (1.0 / l) for (_, l, acc) in carry], axis=0)
    o_ref[...] = (out_t.T * g_ref[...]).astype(BF16)


def _attn_call(q, k, vt, gbs):
    B, S, _ = q.shape
    T = vt.shape[-1]
    nq = S // T
    pair = 2 * HEAD_PAD
    return pl.pallas_call(
        _attn_kernel,
        grid=(B, MLA_HEADS // 2, nq),
        in_specs=[
            pl.BlockSpec((None, T, pair), lambda b, hp, i: (b, i, hp)),
            pl.BlockSpec((None, S, pair), lambda b, hp, i: (b, 0, hp)),
            pl.BlockSpec((None, 2, nq, V_HEAD_DIM, T), lambda b, hp, i: (b, hp, 0, 0, 0)),
            pl.BlockSpec((None, T, 2 * V_HEAD_DIM), lambda b, hp, i: (b, i, hp)),
        ],
        out_specs=pl.BlockSpec((None, T, 2 * V_HEAD_DIM), lambda b, hp, i: (b, i, hp)),
        out_shape=jax.ShapeDtypeStruct((B, S, D_MLA), BF16),
        compiler_params=pltpu.CompilerParams(
            dimension_semantics=("arbitrary", "arbitrary", "arbitrary")),
    )(q, k, vt, gbs)


def _out_kernel(x_ref, ya_ref, yb_ref, yc_ref, w_ref, fg_ref, o_ref, *, final):
    y = (_dot(ya_ref[...], w_ref[0:D_LRU, :])
         + _dot(yb_ref[...], w_ref[D_LRU:D_LRU + D_MLA, :])
         + _dot(yc_ref[...], w_ref[D_LRU + D_MLA:D_MIX, :]))
    xn = x_ref[...] + y
    if final:
        xn = _rms(xn, fg_ref[...])
    o_ref[...] = xn


def _out_call(x, ya, yb, yc, w_out, final_g, final):
    B, S, _ = x.shape
    ts = T_OUT
    row_blk = lambda w: pl.BlockSpec((None, ts, w), lambda b, t: (b, t, 0))
    full = lambda a: pl.BlockSpec(a.shape, lambda b, t: (0,) * a.ndim)
    return pl.pallas_call(
        functools.partial(_out_kernel, final=final),
        grid=(B, S // ts),
        in_specs=[row_blk(D_MODEL), row_blk(D_LRU), row_blk(D_MLA), row_blk(D_POOL),
                  full(w_out), full(final_g)],
        out_specs=row_blk(D_MODEL),
        out_shape=jax.ShapeDtypeStruct((B, S, D_MODEL), F32),
        compiler_params=pltpu.CompilerParams(dimension_semantics=("arbitrary", "arbitrary")),
    )(x, ya, yb, yc, w_out, final_g)


def _block_diag(w):
    dp, g, n, _ = w.shape
    eye = jnp.eye(g, dtype=w.dtype)
    return jnp.einsum("dgij,gh->dgihj", w, eye).reshape(dp, g * n, g * n)


def _pack_weights(norm_g, w_in, conv_w, conv_b, w_rg, b_rg, w_ig, b_ig, lru_lambda,
                  q_norm_g, w_uq, kv_norm_g, w_ukv, w_pool, pool_scale, w_out):
    offs = np.cumsum((0,) + IN_SIZES)
    seg = lambda n: w_in[:, :, offs[n]:offs[n + 1]]
    kr_pad = jnp.pad(seg(4), ((0, 0), (0, 0), (ROPE_LO, HEAD_PAD - ROPE_LO - QK_ROPE_DIM)))
    w_in_p = jnp.concatenate([seg(0), seg(1), seg(2), seg(3), seg(5), seg(6), seg(7), kr_pad],
                             axis=-1).astype(BF16)
    w_gate = jnp.concatenate([_block_diag(w_rg), _block_diag(w_ig)], axis=-1).astype(BF16)
    b_gate = jnp.concatenate([b_rg, b_ig], axis=-1)[:, None, :]
    uq = w_uq.reshape(DEPTH, Q_LORA_RANK, MLA_HEADS, QK_NOPE_DIM + QK_ROPE_DIM)
    uq = jnp.pad(uq, ((0, 0), (0, 0), (0, 0), (0, HEAD_PAD - QK_NOPE_DIM - QK_ROPE_DIM)))
    ukv = w_ukv.reshape(DEPTH, KV_LORA_RANK, MLA_HEADS, QK_NOPE_DIM + V_HEAD_DIM)
    uk = jnp.pad(ukv[..., :QK_NOPE_DIM], ((0, 0), (0, 0), (0, 0), (0, HEAD_PAD - QK_NOPE_DIM)))
    uv = ukv[..., QK_NOPE_DIM:].reshape(DEPTH, KV_LORA_RANK, D_MLA)
    return {
        "norm_g": norm_g[:, None, :],
        "w_in": w_in_p,
        "conv_w": conv_w,
        "conv_b": conv_b[:, None, :],
        "w_gate": w_gate,
        "b_gate": b_gate,
        "lam": lru_lambda[:, None, :],
        "q_norm_g": q_norm_g[:, None, :],
        "w_uq": uq.reshape(DEPTH, Q_LORA_RANK, D_QK).astype(BF16),
        "kv_norm_g": kv_norm_g[:, None, :],
        "w_uk": uk.reshape(DEPTH, KV_LORA_RANK, D_QK).astype(BF16),
        "w_vt": jnp.swapaxes(uv, 1, 2).astype(BF16),
        "w_pool": _block_diag(w_pool).astype(BF16),
        "pool_scale": pool_scale[:, None, :],
        "w_out": w_out.astype(BF16),
    }


def _rope_tables(seq_len):
    pos = jnp.arange(seq_len, dtype=F32)
    inv_freq = ROPE_BASE ** (-jnp.arange(0, QK_ROPE_DIM, 2, dtype=F32) / QK_ROPE_DIM)
    ang = pos[:, None] * inv_freq[None, :]
    cos, sin = jnp.cos(ang), jnp.sin(ang)
    zeros = jnp.zeros_like(sin)
    tail = jnp.zeros((seq_len, HEAD_PAD - ROPE_LO - QK_ROPE_DIM), F32)
    ones = jnp.ones((seq_len, ROPE_LO), F32)
    lead = jnp.zeros((seq_len, ROPE_LO), F32)
    cos_t = jnp.concatenate([ones, cos, cos, tail], axis=1)
    sin_hi = jnp.concatenate([lead, -sin, zeros, tail], axis=1)
    sin_lo = jnp.concatenate([lead, zeros, sin, tail], axis=1)
    return cos_t, sin_hi, sin_lo


def kernel(x, norm_g, w_in, conv_w, conv_b, w_rg, b_rg, w_ig, b_ig, lru_lambda, q_norm_g, w_uq,
           kv_norm_g, w_ukv, w_pool, pool_scale, w_out, final_norm_g):
    B, S, D = x.shape
    assert D == D_MODEL and S % T_PROJ == 0 and S % T_ATT == 0 and T_PROJ % T_ATT == 0
    packed = _pack_weights(norm_g, w_in, conv_w, conv_b, w_rg, b_rg, w_ig, b_ig, lru_lambda,
                           q_norm_g, w_uq, kv_norm_g, w_ukv, w_pool, pool_scale, w_out)
    tabs = _rope_tables(S)
    final_g = final_norm_g[None, :]
    for l in range(DEPTH):
        lw = {n: v[l] for n, v in packed.items()}
        ya, gbs, yc, q, k, vt = _proj_call(x, lw, tabs, T_ATT)
        yb = _attn_call(q, k, vt, gbs)
        x = _out_call(x, ya, yb, yc, lw["w_out"], final_g, final=(l == DEPTH - 1))
    return x
```

```python
import functools

import jax
import jax.numpy as jnp
import numpy as np
from jax import lax
from jax.experimental import pallas as pl
from jax.experimental.pallas import tpu as pltpu

F32 = jnp.float32
BF16 = jnp.bfloat16

D_MODEL = 1024
DEPTH = 4
EPS = 1e-6
D_LRU = 384
LRU_HEADS = 6
LRU_HEAD_DIM = 64
CONV_WIDTH = 4
LRU_C = 8.0
MLA_HEADS = 6
QK_NOPE_DIM = 64
QK_ROPE_DIM = 32
V_HEAD_DIM = 64
D_MLA = MLA_HEADS * V_HEAD_DIM
Q_LORA_RANK = 384
KV_LORA_RANK = 256
ROPE_BASE = 10000.0
POOL_WINDOWS = (2, 4, 8, 16)
POOL_GROUP_DIM = 64
D_POOL = POOL_GROUP_DIM * len(POOL_WINDOWS)
D_MIX = D_LRU + D_MLA + D_POOL
IN_SIZES = (D_LRU, D_LRU, Q_LORA_RANK, KV_LORA_RANK, QK_ROPE_DIM, D_MLA, D_POOL, D_POOL)

LANES = 128
HEAD_PAD = LANES
D_QK = MLA_HEADS * HEAD_PAD
ROPE_LO = QK_NOPE_DIM
HALF = QK_ROPE_DIM // 2
CONV_HALO = 8
POOL_HALO = 16
NEG = -1e30
BF16_SUBLANES = 16
VT_ROWS = V_HEAD_DIM + BF16_SUBLANES
Q_SCALE = (QK_NOPE_DIM + QK_ROPE_DIM) ** -0.5 * float(np.log2(np.e))

C_ZA = 0
C_GA = C_ZA + D_LRU
C_CQ = C_GA + D_LRU
C_CKV = C_CQ + Q_LORA_RANK
C_GB = C_CKV + KV_LORA_RANK
C_ZC = C_GB + D_MLA
C_GC = C_ZC + D_POOL
C_KR = C_GC + D_POOL
D_IN_PAD = C_KR + HEAD_PAD

T_PROJ = 512
T_ATT = 512
T_OUT = 512


def _dot(a, b):
    return jnp.dot(a, b, preferred_element_type=F32)


def _dot_nt(a, b):
    return lax.dot_general(a, b, (((1,), (1,)), ((), ())), preferred_element_type=F32)


def _rms(x, g):
    return x * lax.rsqrt(jnp.mean(x * x, axis=-1, keepdims=True) + EPS) * g


def _silu(x):
    return x * jax.nn.sigmoid(x)


def _rope(x, cos, sin_hi, sin_lo):
    return (x * cos + pltpu.roll(x, LANES - HALF, axis=1) * sin_hi
            + pltpu.roll(x, HALF, axis=1) * sin_lo)


def _proj_kernel(x_ref, ng_ref, win_ref, convw_ref, convb_ref, wg_ref, bg_ref, lam_ref,
                 qg_ref, wuq_ref, kvg_ref, wuk_ref, wvt_ref, wpool_ref, pscale_ref,
                 cos_ref, shi_ref, slo_ref,
                 ya_ref, gbs_ref, yc_ref, q_ref, k_ref, vt_ref,
                 za_ext, zc_ext, h_carry):
    ts = x_ref.shape[0]
    t = pl.program_id(1)

    @pl.when(t == 0)
    def _():
        za_ext[0:CONV_HALO, :] = jnp.zeros((CONV_HALO, D_LRU), F32)
        zc_ext[0:POOL_HALO, :] = jnp.zeros((POOL_HALO, D_POOL), F32)
        h_carry[...] = jnp.zeros_like(h_carry)

    hb = _rms(x_ref[...], ng_ref[...]).astype(BF16)

    zag = _dot(hb, win_ref[:, C_ZA:C_CQ])
    za = zag[:, :D_LRU]
    ga = zag[:, D_LRU:]
    za_ext[CONV_HALO:CONV_HALO + ts, :] = za
    xa = convb_ref[...]
    for k in range(CONV_WIDTH):
        off = CONV_HALO - (CONV_WIDTH - 1) + k
        xa = xa + za_ext[off:off + ts, :] * convw_ref[k:k + 1, :]
    za_ext[0:CONV_HALO, :] = za_ext[ts:ts + CONV_HALO, :]

    gates = jax.nn.sigmoid(_dot(xa.astype(BF16), wg_ref[...]) + bg_ref[...])
    r = gates[:, :D_LRU]
    i = gates[:, D_LRU:]
    nl = -lam_ref[...]
    softplus = jnp.maximum(nl, 0.0) + jnp.log1p(jnp.exp(-jnp.abs(nl)))
    log_a = (-LRU_C) * r * softplus
    a = jnp.exp(log_a)
    h = jnp.sqrt(-jnp.tanh(log_a) * (a * a + 1.0)) * (i * xa)
    row = lax.broadcasted_iota(jnp.int32, (ts, 1), 0)
    d = 1
    while d < ts:
        keep = row >= d
        a_sh = jnp.where(keep, pltpu.roll(a, d, axis=0), 1.0)
        h_sh = jnp.where(keep, pltpu.roll(h, d, axis=0), 0.0)
        h = a * h_sh + h
        a = a * a_sh
        d *= 2
    h = h + a * h_carry[...]
    h_carry[...] = h[ts - 1:ts, :]
    ya_ref[...] = (h * _silu(ga)).astype(BF16)

    cq = _dot(hb, win_ref[:, C_CQ:C_CKV])
    qf = _dot(_rms(cq, qg_ref[...]).astype(BF16), wuq_ref[...])
    ckv = _dot(hb, win_ref[:, C_CKV:C_GB])
    kvn = _rms(ckv, kvg_ref[...]).astype(BF16)
    kn = _dot(kvn, wuk_ref[...])
    vt = _dot_nt(wvt_ref[...], kvn)
    cos = cos_ref[...]
    shi = shi_ref[...]
    slo = slo_ref[...]
    kr = _rope(_dot(hb, win_ref[:, C_KR:D_IN_PAD]), cos, shi, slo)
    for hd in range(MLA_HEADS):
        sl = slice(hd * HEAD_PAD, (hd + 1) * HEAD_PAD)
        q_ref[:, sl] = (_rope(qf[:, sl], cos, shi, slo) * Q_SCALE).astype(BF16)
        k_ref[:, sl] = (kn[:, sl] + kr).astype(BF16)
    tk = vt_ref.shape[-1]
    ones_rows = (lax.broadcasted_iota(jnp.int32, (VT_ROWS - V_HEAD_DIM, tk), 0) == 0).astype(BF16)
    for hd in range(MLA_HEADS):
        for c in range(ts // tk):
            vt_ref[hd, c, 0:V_HEAD_DIM, :] = (
                vt[hd * V_HEAD_DIM:(hd + 1) * V_HEAD_DIM, c * tk:(c + 1) * tk].astype(BF16))
            vt_ref[hd, c, V_HEAD_DIM:VT_ROWS, :] = ones_rows
    gbs_ref[...] = _silu(_dot(hb, win_ref[:, C_GB:C_ZC]))

    zcg = _dot(hb, win_ref[:, C_ZC:C_KR])
    zc = zcg[:, :D_POOL]
    gc = zcg[:, D_POOL:]
    zc_ext[POOL_HALO:POOL_HALO + ts, :] = zc
    lane = lax.broadcasted_iota(jnp.int32, (1, LANES), 1)
    pos1 = (t * ts + row + 1).astype(F32)
    pooled = []
    for blk in range(D_POOL // LANES):
        w_small, w_big = POOL_WINDOWS[2 * blk], POOL_WINDOWS[2 * blk + 1]
        cols = slice(blk * LANES, (blk + 1) * LANES)
        acc = zc[:, cols]
        sums = {}
        for j in range(1, w_big):
            acc = acc + zc_ext[POOL_HALO - j:POOL_HALO - j + ts, cols]
            if j + 1 in (w_small, w_big):
                sums[j + 1] = acc
        first = lane < POOL_GROUP_DIM
        wsum = jnp.where(first, sums[w_small], sums[w_big])
        cnt = jnp.minimum(pos1, jnp.where(first, float(w_small), float(w_big)))
        pooled.append(wsum / cnt - zc[:, cols])
    zc_ext[0:POOL_HALO, :] = zc_ext[ts:ts + POOL_HALO, :]
    pooled = jnp.concatenate(pooled, axis=1).astype(BF16)
    yc_ref[...] = (_dot(pooled, wpool_ref[...]) * pscale_ref[...] * _silu(gc)).astype(BF16)


def _proj_call(x, lw, tabs, t_att):
    B, S, _ = x.shape
    ts = T_PROJ
    nt = S // ts
    row_blk = lambda w: pl.BlockSpec((None, ts, w), lambda b, t: (b, t, 0))
    full = lambda a: pl.BlockSpec(a.shape, lambda b, t: (0,) * a.ndim)
    tab = pl.BlockSpec((ts, LANES), lambda b, t: (t, 0))
    weights = [lw[n] for n in ("norm_g", "w_in", "conv_w", "conv_b", "w_gate", "b_gate", "lam",
                               "q_norm_g", "w_uq", "kv_norm_g", "w_uk", "w_vt", "w_pool", "pool_scale")]
    out_shape = (
        jax.ShapeDtypeStruct((B, S, D_LRU), BF16),
        jax.ShapeDtypeStruct((B, S, D_MLA), F32),
        jax.ShapeDtypeStruct((B, S, D_POOL), BF16),
        jax.ShapeDtypeStruct((B, S, D_QK), BF16),
        jax.ShapeDtypeStruct((B, S, D_QK), BF16),
        jax.ShapeDtypeStruct((B, MLA_HEADS, S // t_att, VT_ROWS, t_att), BF16),
    )
    out_specs = (
        row_blk(D_LRU), row_blk(D_MLA), row_blk(D_POOL), row_blk(D_QK), row_blk(D_QK),
        pl.BlockSpec((None, MLA_HEADS, ts // t_att, VT_ROWS, t_att), lambda b, t: (b, 0, t, 0, 0)),
    )
    return pl.pallas_call(
        _proj_kernel,
        grid=(B, nt),
        in_specs=[row_blk(D_MODEL)] + [full(w) for w in weights] + [tab, tab, tab],
        out_specs=out_specs,
        out_shape=out_shape,
        scratch_shapes=[
            pltpu.VMEM((ts + CONV_HALO, D_LRU), F32),
            pltpu.VMEM((ts + POOL_HALO, D_POOL), F32),
            pltpu.VMEM((1, D_LRU), F32),
        ],
        compiler_params=pltpu.CompilerParams(dimension_semantics=("arbitrary", "arbitrary")),
    )(x, *weights, *tabs)


def _attn_kernel(q_ref, k_ref, vt_ref, g_ref, o_ref, s_a, s_b):
    T = q_ref.shape[0]
    i = pl.program_id(2)
    qs = [q_ref[:, h * HEAD_PAD:(h + 1) * HEAD_PAD] for h in range(2)]

    def scores(j, h):
        start = pl.multiple_of(j * T, T)
        return _dot_nt(k_ref[pl.ds(start, T), h * HEAD_PAD:(h + 1) * HEAD_PAD], qs[h])

    def update(j, h, s, m, acc):
        m_new = jnp.maximum(m, jnp.max(s, axis=0, keepdims=True))
        alpha = jnp.exp2(m - m_new)
        p = jnp.exp2(s - m_new).astype(BF16)
        return m_new, alpha * acc + _dot(vt_ref[h, j], p)

    def stage(j, cur, nxt, state):
        for h in range(2):
            nxt[h] = scores(j + 1, h)
        return tuple(update(j, h, cur[h], *state[h]) for h in range(2))

    def finish(cur, state):
        kv_pos = lax.broadcasted_iota(jnp.int32, (T, T), 0)
        q_pos = lax.broadcasted_iota(jnp.int32, (T, T), 1)
        outs = []
        for h in range(2):
            _, acc = update(i, h, jnp.where(kv_pos <= q_pos, cur[h], NEG), *state[h])
            outs.append(acc[:V_HEAD_DIM] * (1.0 / acc[V_HEAD_DIM:V_HEAD_DIM + 1]))
        o_ref[...] = (jnp.concatenate(outs, axis=0).T * g_ref[...]).astype(BF16)

    for h in range(2):
        s_a[h] = scores(0, h)

    def pair(jj, state):
        return stage(2 * jj + 1, s_b, s_a, stage(2 * jj, s_a, s_b, state))

    init = tuple((jnp.full((1, T), NEG, F32), jnp.zeros((VT_ROWS, T), F32)) for _ in range(2))
    state = lax.fori_loop(0, i // 2, pair, init)

    @pl.when(i % 2 == 1)
    def _():
        finish(s_b, stage(i - 1, s_a, s_b, state))

    @pl.when(i % 2 == 0)
    def _():
        finish(s_a, state)


def _attn_call(q, k, vt, gbs):
    B, S, _ = q.shape
    T = vt.shape[-1]
    nq = S // T
    pair = 2 * HEAD_PAD
    return pl.pallas_call(
        _attn_kernel,
        grid=(B, MLA_HEADS // 2, nq),
        in_specs=[
            pl.BlockSpec((None, T, pair), lambda b, hp, i: (b, i, hp)),
            pl.BlockSpec((None, S, pair), lambda b, hp, i: (b, 0, hp)),
            pl.BlockSpec((None, 2, nq, VT_ROWS, T), lambda b, hp, i: (b, hp, 0, 0, 0)),
            pl.BlockSpec((None, T, 2 * V_HEAD_DIM), lambda b, hp, i: (b, i, hp)),
        ],
        out_specs=pl.BlockSpec((None, T, 2 * V_HEAD_DIM), lambda b, hp, i: (b, i, hp)),
        out_shape=jax.ShapeDtypeStruct((B, S, D_MLA), BF16),
        scratch_shapes=[pltpu.VMEM((2, T, T), F32), pltpu.VMEM((2, T, T), F32)],
        compiler_params=pltpu.CompilerParams(
            dimension_semantics=("arbitrary", "arbitrary", "arbitrary")),
    )(q, k, vt, gbs)


def _out_kernel(x_ref, ya_ref, yb_ref, yc_ref, w_ref, fg_ref, o_ref, *, final):
    y = (_dot(ya_ref[...], w_ref[0:D_LRU, :])
         + _dot(yb_ref[...], w_ref[D_LRU:D_LRU + D_MLA, :])
         + _dot(yc_ref[...], w_ref[D_LRU + D_MLA:D_MIX, :]))
    xn = x_ref[...] + y
    if final:
        xn = _rms(xn, fg_ref[...])
    o_ref[...] = xn


def _out_call(x, ya, yb, yc, w_out, final_g, final):
    B, S, _ = x.shape
    ts = T_OUT
    row_blk = lambda w: pl.BlockSpec((None, ts, w), lambda b, t: (b, t, 0))
    full = lambda a: pl.BlockSpec(a.shape, lambda b, t: (0,) * a.ndim)
    return pl.pallas_call(
        functools.partial(_out_kernel, final=final),
        grid=(B, S // ts),
        in_specs=[row_blk(D_MODEL), row_blk(D_LRU), row_blk(D_MLA), row_blk(D_POOL),
                  full(w_out), full(final_g)],
        out_specs=row_blk(D_MODEL),
        out_shape=jax.ShapeDtypeStruct((B, S, D_MODEL), F32),
        compiler_params=pltpu.CompilerParams(dimension_semantics=("arbitrary", "arbitrary")),
    )(x, ya, yb, yc, w_out, final_g)


def _block_diag(w):
    dp, g, n, _ = w.shape
    eye = jnp.eye(g, dtype=w.dtype)
    return jnp.einsum("dgij,gh->dgihj", w, eye).reshape(dp, g * n, g * n)


def _pack_weights(norm_g, w_in, conv_w, conv_b, w_rg, b_rg, w_ig, b_ig, lru_lambda,
                  q_norm_g, w_uq, kv_norm_g, w_ukv, w_pool, pool_scale, w_out):
    offs = np.cumsum((0,) + IN_SIZES)
    seg = lambda n: w_in[:, :, offs[n]:offs[n + 1]]
    kr_pad = jnp.pad(seg(4), ((0, 0), (0, 0), (ROPE_LO, HEAD_PAD - ROPE_LO - QK_ROPE_DIM)))
    w_in_p = jnp.concatenate([seg(0), seg(1), seg(2), seg(3), seg(5), seg(6), seg(7), kr_pad],
                             axis=-1).astype(BF16)
    w_gate = jnp.concatenate([_block_diag(w_rg), _block_diag(w_ig)], axis=-1).astype(BF16)
    b_gate = jnp.concatenate([b_rg, b_ig], axis=-1)[:, None, :]
    uq = w_uq.reshape(DEPTH, Q_LORA_RANK, MLA_HEADS, QK_NOPE_DIM + QK_ROPE_DIM)
    uq = jnp.pad(uq, ((0, 0), (0, 0), (0, 0), (0, HEAD_PAD - QK_NOPE_DIM - QK_ROPE_DIM)))
    ukv = w_ukv.reshape(DEPTH, KV_LORA_RANK, MLA_HEADS, QK_NOPE_DIM + V_HEAD_DIM)
    uk = jnp.pad(ukv[..., :QK_NOPE_DIM], ((0, 0), (0, 0), (0, 0), (0, HEAD_PAD - QK_NOPE_DIM)))
    uv = ukv[..., QK_NOPE_DIM:].reshape(DEPTH, KV_LORA_RANK, D_MLA)
    return {
        "norm_g": norm_g[:, None, :],
        "w_in": w_in_p,
        "conv_w": conv_w,
        "conv_b": conv_b[:, None, :],
        "w_gate": w_gate,
        "b_gate": b_gate,
        "lam": lru_lambda[:, None, :],
        "q_norm_g": q_norm_g[:, None, :],
        "w_uq": uq.reshape(DEPTH, Q_LORA_RANK, D_QK).astype(BF16),
        "kv_norm_g": kv_norm_g[:, None, :],
        "w_uk": uk.reshape(DEPTH, KV_LORA_RANK, D_QK).astype(BF16),
        "w_vt": jnp.swapaxes(uv, 1, 2).astype(BF16),
        "w_pool": _block_diag(w_pool).astype(BF16),
        "pool_scale": pool_scale[:, None, :],
        "w_out": w_out.astype(BF16),
    }


def _rope_tables(seq_len):
    pos = jnp.arange(seq_len, dtype=F32)
    inv_freq = ROPE_BASE ** (-jnp.arange(0, QK_ROPE_DIM, 2, dtype=F32) / QK_ROPE_DIM)
    ang = pos[:, None] * inv_freq[None, :]
    cos, sin = jnp.cos(ang), jnp.sin(ang)
    zeros = jnp.zeros_like(sin)
    tail = jnp.zeros((seq_len, HEAD_PAD - ROPE_LO - QK_ROPE_DIM), F32)
    ones = jnp.ones((seq_len, ROPE_LO), F32)
    lead = jnp.zeros((seq_len, ROPE_LO), F32)
    cos_t = jnp.concatenate([ones, cos, cos, tail], axis=1)
    sin_hi = jnp.concatenate([lead, -sin, zeros, tail], axis=1)
    sin_lo = jnp.concatenate([lead, zeros, sin, tail], axis=1)
    return cos_t, sin_hi, sin_lo


def kernel(x, norm_g, w_in, conv_w, conv_b, w_rg, b_rg, w_ig, b_ig, lru_lambda, q_norm_g, w_uq,
           kv_norm_g, w_ukv, w_pool, pool_scale, w_out, final_norm_g):
    B, S, D = x.shape
    assert D == D_MODEL and S % T_PROJ == 0 and S % T_ATT == 0 and T_PROJ % T_ATT == 0
    packed = _pack_weights(norm_g, w_in, conv_w, conv_b, w_rg, b_rg, w_ig, b_ig, lru_lambda,
                           q_norm_g, w_uq, kv_norm_g, w_ukv, w_pool, pool_scale, w_out)
    tabs = _rope_tables(S)
    final_g = final_norm_g[None, :]
    for l in range(DEPTH):
        lw = {n: v[l] for n, v in packed.items()}
        ya, gbs, yc, q, k, vt = _proj_call(x, lw, tabs, T_ATT)
        yb = _attn_call(q, k, vt, gbs)
        x = _out_call(x, ya, yb, yc, lw["w_out"], final_g, final=(l == DEPTH - 1))
    return x
```

```python
import functools
import itertools

import jax
import jax.numpy as jnp
import numpy as np
from jax import lax
from jax.experimental import pallas as pl
from jax.experimental.pallas import tpu as pltpu

F32 = jnp.float32
BF16 = jnp.bfloat16

D_MODEL = 1024
DEPTH = 4
EPS = 1e-6
D_LRU = 384
LRU_HEADS = 6
LRU_HEAD_DIM = 64
CONV_WIDTH = 4
LRU_C = 8.0
MLA_HEADS = 6
QK_NOPE_DIM = 64
QK_ROPE_DIM = 32
V_HEAD_DIM = 64
D_MLA = MLA_HEADS * V_HEAD_DIM
Q_LORA_RANK = 384
KV_LORA_RANK = 256
ROPE_BASE = 10000.0
POOL_WINDOWS = (2, 4, 8, 16)
POOL_GROUP_DIM = 64
D_POOL = POOL_GROUP_DIM * len(POOL_WINDOWS)
D_MIX = D_LRU + D_MLA + D_POOL
IN_SIZES = (D_LRU, D_LRU, Q_LORA_RANK, KV_LORA_RANK, QK_ROPE_DIM, D_MLA, D_POOL, D_POOL)

LANES = 128
SUBLANES = 8
HEAD_PAD = LANES
D_QK = MLA_HEADS * HEAD_PAD
ROPE_LO = QK_NOPE_DIM
HALF = QK_ROPE_DIM // 2
CONV_HALO = 8
POOL_HALO = 16
NEG = -1e30
BF16_SUBLANES = 16
VT_ROWS = V_HEAD_DIM + BF16_SUBLANES
Q_SCALE = (QK_NOPE_DIM + QK_ROPE_DIM) ** -0.5 * float(np.log2(np.e))

C_ZA = 0
C_GA = C_ZA + D_LRU
C_CQ = C_GA + D_LRU
C_CKV = C_CQ + Q_LORA_RANK
C_GB = C_CKV + KV_LORA_RANK
C_ZC = C_GB + D_MLA
C_GC = C_ZC + D_POOL
C_KR = C_GC + D_POOL
D_IN_PAD = C_KR + HEAD_PAD

T_PROJ = 512
T_ATT = 512
T_OUT = 512
Q_CHUNK = 256
Q_TILES_PER_STEP = 4


def _dot(a, b):
    return jnp.dot(a, b, preferred_element_type=F32)


def _dot_nt(a, b):
    return lax.dot_general(a, b, (((1,), (1,)), ((), ())), preferred_element_type=F32)


def _rms(x, g):
    return x * lax.rsqrt(jnp.mean(x * x, axis=-1, keepdims=True) + EPS) * g


def _silu(x):
    return x * jax.nn.sigmoid(x)


def _rope(x, cos, sin_hi, sin_lo):
    return (x * cos + pltpu.roll(x, LANES - HALF, axis=1) * sin_hi
            + pltpu.roll(x, HALF, axis=1) * sin_lo)


def _proj_kernel(x_ref, ng_ref, win_ref, convw_ref, convb_ref, wg_ref, bg_ref, lam_ref,
                 qg_ref, wuq_ref, kvg_ref, wuk_ref, wvt_ref, wpool_ref, pscale_ref,
                 cos_ref, shi_ref, slo_ref,
                 ya_ref, gbs_ref, yc_ref, q_ref, k_ref, vt_ref,
                 za_ext, zc_ext, h_carry):
    ts = x_ref.shape[0]
    t = pl.program_id(1)

    @pl.when(t == 0)
    def _():
        za_ext[0:CONV_HALO, :] = jnp.zeros((CONV_HALO, D_LRU), F32)
        zc_ext[0:POOL_HALO, :] = jnp.zeros((POOL_HALO, D_POOL), F32)
        h_carry[...] = jnp.zeros_like(h_carry)

    hb = _rms(x_ref[...], ng_ref[...]).astype(BF16)

    zag = _dot(hb, win_ref[:, C_ZA:C_CQ])
    za = zag[:, :D_LRU]
    ga = zag[:, D_LRU:]
    za_ext[CONV_HALO:CONV_HALO + ts, :] = za
    xa = convb_ref[...]
    for k in range(CONV_WIDTH):
        off = CONV_HALO - (CONV_WIDTH - 1) + k
        xa = xa + za_ext[off:off + ts, :] * convw_ref[k:k + 1, :]
    za_ext[0:CONV_HALO, :] = za_ext[ts:ts + CONV_HALO, :]
    gate_pre = _dot(xa.astype(BF16), wg_ref[...]) + bg_ref[...]

    cq = _dot(hb, win_ref[:, C_CQ:C_CKV])
    gates = jax.nn.sigmoid(gate_pre)
    r = gates[:, :D_LRU]
    i = gates[:, D_LRU:]
    nl = -lam_ref[...]
    softplus = jnp.maximum(nl, 0.0) + jnp.log1p(jnp.exp(-jnp.abs(nl)))
    log_a = (-LRU_C) * r * softplus
    a = jnp.exp(log_a)
    h = jnp.sqrt(-jnp.tanh(log_a) * (a * a + 1.0)) * (i * xa)

    ckv = _dot(hb, win_ref[:, C_CKV:C_GB])
    qf = _dot(_rms(cq, qg_ref[...]).astype(BF16), wuq_ref[...])
    kvn = _rms(ckv, kvg_ref[...]).astype(BF16)

    row = lax.broadcasted_iota(jnp.int32, (ts, 1), 0)
    matmuls = iter((
        lambda: _dot(kvn, wuk_ref[...]),
        lambda: _dot_nt(wvt_ref[...], kvn),
        lambda: _dot(hb, win_ref[:, C_KR:D_IN_PAD]),
        lambda: _dot(hb, win_ref[:, C_GB:C_ZC]),
        lambda: _dot(hb, win_ref[:, C_ZC:C_KR]),
    ))
    products = []
    d = 1
    while d < ts:
        if d < SUBLANES:
            keep = row >= d
            a_sh = jnp.where(keep, pltpu.roll(a, d, axis=0), 1.0)
            h_sh = jnp.where(keep, pltpu.roll(h, d, axis=0), 0.0)
            h = a * h_sh + h
            a = a * a_sh
        else:
            h = jnp.concatenate([h[:d], a[d:] * h[:ts - d] + h[d:]], axis=0)
            a = jnp.concatenate([a[:d], a[d:] * a[:ts - d]], axis=0)
        if d >= SUBLANES // 2:
            products.extend(f() for f in itertools.islice(matmuls, 1))
        d *= 2
    kn, vt, kr_pre, gb, zcg = products + [f() for f in matmuls]
    h = h + a * h_carry[...]
    h_carry[...] = h[ts - 1:ts, :]
    ya_ref[...] = (h * _silu(ga)).astype(BF16)

    cos = cos_ref[...]
    shi = shi_ref[...]
    slo = slo_ref[...]
    kr = _rope(kr_pre, cos, shi, slo)
    for hd in range(MLA_HEADS):
        sl = slice(hd * HEAD_PAD, (hd + 1) * HEAD_PAD)
        q_ref[:, sl] = (_rope(qf[:, sl], cos, shi, slo) * Q_SCALE).astype(BF16)
        k_ref[:, sl] = (kn[:, sl] + kr).astype(BF16)
    tk = vt_ref.shape[-1]
    ones_rows = (lax.broadcasted_iota(jnp.int32, (VT_ROWS - V_HEAD_DIM, tk), 0) == 0).astype(BF16)
    for hd in range(MLA_HEADS):
        for c in range(ts // tk):
            vt_ref[hd, c, 0:V_HEAD_DIM, :] = (
                vt[hd * V_HEAD_DIM:(hd + 1) * V_HEAD_DIM, c * tk:(c + 1) * tk].astype(BF16))
            vt_ref[hd, c, V_HEAD_DIM:VT_ROWS, :] = ones_rows
    gbs_ref[...] = _silu(gb)

    zc = zcg[:, :D_POOL]
    gc = zcg[:, D_POOL:]
    zc_ext[POOL_HALO:POOL_HALO + ts, :] = zc
    lane = lax.broadcasted_iota(jnp.int32, (1, LANES), 1)
    pos1 = (t * ts + row + 1).astype(F32)
    pooled = []
    for blk in range(D_POOL // LANES):
        w_small, w_big = POOL_WINDOWS[2 * blk], POOL_WINDOWS[2 * blk + 1]
        cols = slice(blk * LANES, (blk + 1) * LANES)
        acc = zc[:, cols]
        sums = {}
        for j in range(1, w_big):
            acc = acc + zc_ext[POOL_HALO - j:POOL_HALO - j + ts, cols]
            if j + 1 in (w_small, w_big):
                sums[j + 1] = acc
        first = lane < POOL_GROUP_DIM
        wsum = jnp.where(first, sums[w_small], sums[w_big])
        cnt = jnp.minimum(pos1, jnp.where(first, float(w_small), float(w_big)))
        pooled.append(wsum / cnt - zc[:, cols])
    zc_ext[0:POOL_HALO, :] = zc_ext[ts:ts + POOL_HALO, :]
    pooled = jnp.concatenate(pooled, axis=1).astype(BF16)
    yc_ref[...] = (_dot(pooled, wpool_ref[...]) * pscale_ref[...] * _silu(gc)).astype(BF16)


def _proj_call(x, lw, tabs, t_att):
    B, S, _ = x.shape
    ts = T_PROJ
    nt = S // ts
    row_blk = lambda w: pl.BlockSpec((None, ts, w), lambda b, t: (b, t, 0))
    full = lambda a: pl.BlockSpec(a.shape, lambda b, t: (0,) * a.ndim)
    tab = pl.BlockSpec((ts, LANES), lambda b, t: (t, 0))
    weights = [lw[n] for n in ("norm_g", "w_in", "conv_w", "conv_b", "w_gate", "b_gate", "lam",
                               "q_norm_g", "w_uq", "kv_norm_g", "w_uk", "w_vt", "w_pool", "pool_scale")]
    out_shape = (
        jax.ShapeDtypeStruct((B, S, D_LRU), BF16),
        jax.ShapeDtypeStruct((B, S, D_MLA), F32),
        jax.ShapeDtypeStruct((B, S, D_POOL), BF16),
        jax.ShapeDtypeStruct((B, S, D_QK), BF16),
        jax.ShapeDtypeStruct((B, S, D_QK), BF16),
        jax.ShapeDtypeStruct((B, MLA_HEADS, S // t_att, VT_ROWS, t_att), BF16),
    )
    out_specs = (
        row_blk(D_LRU), row_blk(D_MLA), row_blk(D_POOL), row_blk(D_QK), row_blk(D_QK),
        pl.BlockSpec((None, MLA_HEADS, ts // t_att, VT_ROWS, t_att), lambda b, t: (b, 0, t, 0, 0)),
    )
    return pl.pallas_call(
        _proj_kernel,
        grid=(B, nt),
        in_specs=[row_blk(D_MODEL)] + [full(w) for w in weights] + [tab, tab, tab],
        out_specs=out_specs,
        out_shape=out_shape,
        scratch_shapes=[
            pltpu.VMEM((ts + CONV_HALO, D_LRU), F32),
            pltpu.VMEM((ts + POOL_HALO, D_POOL), F32),
            pltpu.VMEM((1, D_LRU), F32),
        ],
        compiler_params=pltpu.CompilerParams(dimension_semantics=("arbitrary", "arbitrary")),
    )(x, *weights, *tabs)


def _attn_kernel(q_ref, k_ref, vt_ref, g_ref, o_ref, s_a, s_b):
    T = vt_ref.shape[-1]
    g = pl.program_id(2)
    n_c = T // Q_CHUNK
    chains = [(h, c) for h in range(2) for c in range(n_c)]
    n_chains = len(chains)

    def score_chain(r, j, n, nxt):
        h, c = chains[n]
        q = q_ref[r * T + c * Q_CHUNK:r * T + (c + 1) * Q_CHUNK, h * HEAD_PAD:(h + 1) * HEAD_PAD]
        s = _dot_nt(k_ref[pl.ds(pl.multiple_of(j * T, T), T), h * HEAD_PAD:(h + 1) * HEAD_PAD], q)
        nxt[h, :, c * Q_CHUNK:(c + 1) * Q_CHUNK] = s
        return jnp.max(s, axis=0, keepdims=True)

    def update_chain(j, n, cur, state, cmax, diag):
        h, c = chains[n]
        m, acc = state[n]
        rows = (c + 1) * Q_CHUNK if diag else T

        def load():
            s = cur[h, 0:rows, c * Q_CHUNK:(c + 1) * Q_CHUNK]
            if diag:
                kv_pos = lax.broadcasted_iota(jnp.int32, (rows, Q_CHUNK), 0)
                q_pos = lax.broadcasted_iota(jnp.int32, (rows, Q_CHUNK), 1) + c * Q_CHUNK
                s = jnp.where(kv_pos <= q_pos, s, NEG)
            return s

        tile_max = jnp.max(load(), axis=0, keepdims=True) if diag else cmax[n]
        m_new = jnp.maximum(m, tile_max)
        alpha = jnp.exp2(m - m_new)
        p = jnp.exp2(load() - m_new).astype(BF16)
        return m_new, alpha * acc + _dot(vt_ref[h, j, :, 0:rows], p)

    def stage(r, j, cur, nxt, carry):
        state, cmax = carry
        new_state, new_cmax = [], []
        for n in range(n_chains):
            new_cmax.append(score_chain(r, j + 1, n, nxt))
            new_state.append(update_chain(j, n, cur, state, cmax, False))
        return tuple(new_state), tuple(new_cmax)

    def finish(r, j, cur, carry, nxt):
        state, new_cmax = [], []
        for n in range(n_chains):
            if nxt is not None:
                new_cmax.append(score_chain(r + 1, 0, n, nxt))
            state.append(update_chain(j, n, cur, carry[0], None, True))
        outs = [acc[:V_HEAD_DIM] * (1.0 / acc[V_HEAD_DIM:V_HEAD_DIM + 1]) for _, acc in state]
        out_t = jnp.concatenate(
            [jnp.concatenate(outs[h * n_c:(h + 1) * n_c], axis=1) for h in range(2)], axis=0)
        o_ref[r * T:(r + 1) * T, :] = (out_t.T * g_ref[r * T:(r + 1) * T, :]).astype(BF16)
        return tuple(new_cmax)

    init = tuple((jnp.full((1, Q_CHUNK), NEG, F32), jnp.zeros((VT_ROWS, Q_CHUNK), F32))
                 for _ in chains)
    cmax = tuple(score_chain(0, 0, n, s_a) for n in range(n_chains))
    first, other = s_a, s_b
    for r in range(Q_TILES_PER_STEP):
        n_full = Q_TILES_PER_STEP * g + r
        x, y = first, other

        def pair(jj, carry, r=r, x=x, y=y):
            return stage(r, 2 * jj + 1, y, x, stage(r, 2 * jj, x, y, carry))

        carry = lax.fori_loop(0, n_full // 2, pair, (init, cmax))
        if r % 2:
            carry = stage(r, n_full - 1, x, y, carry)
            x, y = y, x
        last = r == Q_TILES_PER_STEP - 1
        cmax = finish(r, n_full, x, carry, None if last else y)
        first, other = y, x


def _attn_call(q, k, vt, gbs):
    B, S, _ = q.shape
    T = vt.shape[-1]
    nq = S // T
    rows = Q_TILES_PER_STEP * T
    pair = 2 * HEAD_PAD
    return pl.pallas_call(
        _attn_kernel,
        grid=(B, MLA_HEADS // 2, nq // Q_TILES_PER_STEP),
        in_specs=[
            pl.BlockSpec((None, rows, pair), lambda b, hp, g: (b, g, hp)),
            pl.BlockSpec((None, S, pair), lambda b, hp, g: (b, 0, hp)),
            pl.BlockSpec((None, 2, nq, VT_ROWS, T), lambda b, hp, g: (b, hp, 0, 0, 0)),
            pl.BlockSpec((None, rows, 2 * V_HEAD_DIM), lambda b, hp, g: (b, g, hp)),
        ],
        out_specs=pl.BlockSpec((None, rows, 2 * V_HEAD_DIM), lambda b, hp, g: (b, g, hp)),
        out_shape=jax.ShapeDtypeStruct((B, S, D_MLA), BF16),
        scratch_shapes=[pltpu.VMEM((2, T, T), F32), pltpu.VMEM((2, T, T), F32)],
        compiler_params=pltpu.CompilerParams(
            dimension_semantics=("arbitrary", "arbitrary", "arbitrary")),
    )(q, k, vt, gbs)


def _out_kernel(x_ref, ya_ref, yb_ref, yc_ref, w_ref, fg_ref, o_ref, *, final):
    y = (_dot(ya_ref[...], w_ref[0:D_LRU, :])
         + _dot(yb_ref[...], w_ref[D_LRU:D_LRU + D_MLA, :])
         + _dot(yc_ref[...], w_ref[D_LRU + D_MLA:D_MIX, :]))
    xn = x_ref[...] + y
    if final:
        xn = _rms(xn, fg_ref[...])
    o_ref[...] = xn


def _out_call(x, ya, yb, yc, w_out, final_g, final):
    B, S, _ = x.shape
    ts = T_OUT
    row_blk = lambda w: pl.BlockSpec((None, ts, w), lambda b, t: (b, t, 0))
    full = lambda a: pl.BlockSpec(a.shape, lambda b, t: (0,) * a.ndim)
    return pl.pallas_call(
        functools.partial(_out_kernel, final=final),
        grid=(B, S // ts),
        in_specs=[row_blk(D_MODEL), row_blk(D_LRU), row_blk(D_MLA), row_blk(D_POOL),
                  full(w_out), full(final_g)],
        out_specs=row_blk(D_MODEL),
        out_shape=jax.ShapeDtypeStruct((B, S, D_MODEL), F32),
        compiler_params=pltpu.CompilerParams(dimension_semantics=("arbitrary", "arbitrary")),
    )(x, ya, yb, yc, w_out, final_g)


def _block_diag(w):
    dp, g, n, _ = w.shape
    eye = jnp.eye(g, dtype=w.dtype)
    return jnp.einsum("dgij,gh->dgihj", w, eye).reshape(dp, g * n, g * n)


def _pack_weights(norm_g, w_in, conv_w, conv_b, w_rg, b_rg, w_ig, b_ig, lru_lambda,
                  q_norm_g, w_uq, kv_norm_g, w_ukv, w_pool, pool_scale, w_out):
    offs = np.cumsum((0,) + IN_SIZES)
    seg = lambda n: w_in[:, :, offs[n]:offs[n + 1]]
    kr_pad = jnp.pad(seg(4), ((0, 0), (0, 0), (ROPE_LO, HEAD_PAD - ROPE_LO - QK_ROPE_DIM)))
    w_in_p = jnp.concatenate([seg(0), seg(1), seg(2), seg(3), seg(5), seg(6), seg(7), kr_pad],
                             axis=-1).astype(BF16)
    w_gate = jnp.concatenate([_block_diag(w_rg), _block_diag(w_ig)], axis=-1).astype(BF16)
    b_gate = jnp.concatenate([b_rg, b_ig], axis=-1)[:, None, :]
    uq = w_uq.reshape(DEPTH, Q_LORA_RANK, MLA_HEADS, QK_NOPE_DIM + QK_ROPE_DIM)
    uq = jnp.pad(uq, ((0, 0), (0, 0), (0, 0), (0, HEAD_PAD - QK_NOPE_DIM - QK_ROPE_DIM)))
    ukv = w_ukv.reshape(DEPTH, KV_LORA_RANK, MLA_HEADS, QK_NOPE_DIM + V_HEAD_DIM)
    uk = jnp.pad(ukv[..., :QK_NOPE_DIM], ((0, 0), (0, 0), (0, 0), (0, HEAD_PAD - QK_NOPE_DIM)))
    uv = ukv[..., QK_NOPE_DIM:].reshape(DEPTH, KV_LORA_RANK, D_MLA)
    return {
        "norm_g": norm_g[:, None, :],
        "w_in": w_in_p,
        "conv_w": conv_w,
        "conv_b": conv_b[:, None, :],
        "w_gate": w_gate,
        "b_gate": b_gate,
        "lam": lru_lambda[:, None, :],
        "q_norm_g": q_norm_g[:, None, :],
        "w_uq": uq.reshape(DEPTH, Q_LORA_RANK, D_QK).astype(BF16),
        "kv_norm_g": kv_norm_g[:, None, :],
        "w_uk": uk.reshape(DEPTH, KV_LORA_RANK, D_QK).astype(BF16),
        "w_vt": jnp.swapaxes(uv, 1, 2).astype(BF16),
        "w_pool": _block_diag(w_pool).astype(BF16),
        "pool_scale": pool_scale[:, None, :],
        "w_out": w_out.astype(BF16),
    }


def _rope_tables(seq_len):
    pos = jnp.arange(seq_len, dtype=F32)
    inv_freq = ROPE_BASE ** (-jnp.arange(0, QK_ROPE_DIM, 2, dtype=F32) / QK_ROPE_DIM)
    ang = pos[:, None] * inv_freq[None, :]
    cos, sin = jnp.cos(ang), jnp.sin(ang)
    zeros = jnp.zeros_like(sin)
    tail = jnp.zeros((seq_len, HEAD_PAD - ROPE_LO - QK_ROPE_DIM), F32)
    ones = jnp.ones((seq_len, ROPE_LO), F32)
    lead = jnp.zeros((seq_len, ROPE_LO), F32)
    cos_t = jnp.concatenate([ones, cos, cos, tail], axis=1)
    sin_hi = jnp.concatenate([lead, -sin, zeros, tail], axis=1)
    sin_lo = jnp.concatenate([lead, zeros, sin, tail], axis=1)
    return cos_t, sin_hi, sin_lo


def kernel(x, norm_g, w_in, conv_w, conv_b, w_rg, b_rg, w_ig, b_ig, lru_lambda, q_norm_g, w_uq,
           kv_norm_g, w_ukv, w_pool, pool_scale, w_out, final_norm_g):
    B, S, D = x.shape
    assert D == D_MODEL and S % T_PROJ == 0 and S % T_ATT == 0 and T_PROJ % T_ATT == 0
    packed = _pack_weights(norm_g, w_in, conv_w, conv_b, w_rg, b_rg, w_ig, b_ig, lru_lambda,
                           q_norm_g, w_uq, kv_norm_g, w_ukv, w_pool, pool_scale, w_out)
    tabs = _rope_tables(S)
    final_g = final_norm_g[None, :]
    for l in range(DEPTH):
        lw = {n: v[l] for n, v in packed.items()}
        ya, gbs, yc, q, k, vt = _proj_call(x, lw, tabs, T_ATT)
        yb = _attn_call(q, k, vt, gbs)
        x = _out_call(x, ya, yb, yc, lw["w_out"], final_g, final=(l == DEPTH - 1))
    return x
```

```python
import functools
import itertools

import jax
import jax.numpy as jnp
import numpy as np
from jax import lax
from jax.experimental import pallas as pl
from jax.experimental.pallas import tpu as pltpu

F32 = jnp.float32
BF16 = jnp.bfloat16

D_MODEL = 1024
DEPTH = 4
EPS = 1e-6
D_LRU = 384
LRU_HEADS = 6
LRU_HEAD_DIM = 64
CONV_WIDTH = 4
LRU_C = 8.0
MLA_HEADS = 6
QK_NOPE_DIM = 64
QK_ROPE_DIM = 32
V_HEAD_DIM = 64
D_MLA = MLA_HEADS * V_HEAD_DIM
Q_LORA_RANK = 384
KV_LORA_RANK = 256
ROPE_BASE = 10000.0
POOL_WINDOWS = (2, 4, 8, 16)
POOL_GROUP_DIM = 64
D_POOL = POOL_GROUP_DIM * len(POOL_WINDOWS)
D_MIX = D_LRU + D_MLA + D_POOL
IN_SIZES = (D_LRU, D_LRU, Q_LORA_RANK, KV_LORA_RANK, QK_ROPE_DIM, D_MLA, D_POOL, D_POOL)

LANES = 128
SUBLANES = 8
HEAD_PAD = LANES
D_QK = MLA_HEADS * HEAD_PAD
ROPE_LO = QK_NOPE_DIM
HALF = QK_ROPE_DIM // 2
CONV_HALO = 8
POOL_HALO = 16
NEG = -1e30
BF16_SUBLANES = 16
VT_ROWS = V_HEAD_DIM + BF16_SUBLANES
Q_SCALE = (QK_NOPE_DIM + QK_ROPE_DIM) ** -0.5 * float(np.log2(np.e))

C_ZA = 0
C_GA = C_ZA + D_LRU
C_CQ = C_GA + D_LRU
C_CKV = C_CQ + Q_LORA_RANK
C_GB = C_CKV + KV_LORA_RANK
C_ZC = C_GB + D_MLA
C_GC = C_ZC + D_POOL
C_KR = C_GC + D_POOL
D_IN_PAD = C_KR + HEAD_PAD

T_PROJ = 512
T_ATT = 512
T_OUT = 512
Q_CHUNK = 256
Q_TILES_PER_STEP = 4
assert Q_TILES_PER_STEP == 4


def _dot(a, b):
    return jnp.dot(a, b, preferred_element_type=F32)


def _dot_nt(a, b):
    return lax.dot_general(a, b, (((1,), (1,)), ((), ())), preferred_element_type=F32)


def _rms(x, g):
    return x * lax.rsqrt(jnp.mean(x * x, axis=-1, keepdims=True) + EPS) * g


def _silu(x):
    return x * jax.nn.sigmoid(x)


def _rope(x, cos, sin_hi, sin_lo):
    return (x * cos + pltpu.roll(x, LANES - HALF, axis=1) * sin_hi
            + pltpu.roll(x, HALF, axis=1) * sin_lo)


def _proj_kernel(x_ref, ng_ref, win_ref, convw_ref, convb_ref, wg_ref, bg_ref, lam_ref,
                 qg_ref, wuq_ref, kvg_ref, wuk_ref, wvt_ref, wpool_ref, pscale_ref,
                 cos_ref, shi_ref, slo_ref,
                 ya_ref, gbs_ref, yc_ref, q_ref, k_ref, vt_ref,
                 za_ext, zc_ext, h_carry):
    ts = x_ref.shape[0]
    t = pl.program_id(1)

    @pl.when(t == 0)
    def _():
        za_ext[0:CONV_HALO, :] = jnp.zeros((CONV_HALO, D_LRU), F32)
        zc_ext[0:POOL_HALO, :] = jnp.zeros((POOL_HALO, D_POOL), F32)
        h_carry[...] = jnp.zeros_like(h_carry)

    hb = _rms(x_ref[...], ng_ref[...]).astype(BF16)

    zag = _dot(hb, win_ref[:, C_ZA:C_CQ])
    za = zag[:, :D_LRU]
    ga = zag[:, D_LRU:]
    za_ext[CONV_HALO:CONV_HALO + ts, :] = za
    xa = convb_ref[...]
    for k in range(CONV_WIDTH):
        off = CONV_HALO - (CONV_WIDTH - 1) + k
        xa = xa + za_ext[off:off + ts, :] * convw_ref[k:k + 1, :]
    za_ext[0:CONV_HALO, :] = za_ext[ts:ts + CONV_HALO, :]
    gate_pre = _dot(xa.astype(BF16), wg_ref[...]) + bg_ref[...]

    cq = _dot(hb, win_ref[:, C_CQ:C_CKV])
    gates = jax.nn.sigmoid(gate_pre)
    r = gates[:, :D_LRU]
    i = gates[:, D_LRU:]
    nl = -lam_ref[...]
    softplus = jnp.maximum(nl, 0.0) + jnp.log1p(jnp.exp(-jnp.abs(nl)))
    log_a = (-LRU_C) * r * softplus
    a = jnp.exp(log_a)
    h = jnp.sqrt(-jnp.tanh(log_a) * (a * a + 1.0)) * (i * xa)

    ckv = _dot(hb, win_ref[:, C_CKV:C_GB])
    qf = _dot(_rms(cq, qg_ref[...]).astype(BF16), wuq_ref[...])
    kvn = _rms(ckv, kvg_ref[...]).astype(BF16)

    row = lax.broadcasted_iota(jnp.int32, (ts, 1), 0)
    matmuls = iter((
        lambda: _dot(kvn, wuk_ref[...]),
        lambda: _dot_nt(wvt_ref[...], kvn),
        lambda: _dot(hb, win_ref[:, C_KR:D_IN_PAD]),
        lambda: _dot(hb, win_ref[:, C_GB:C_ZC]),
        lambda: _dot(hb, win_ref[:, C_ZC:C_KR]),
    ))
    products = []
    d = 1
    while d < ts:
        if d < SUBLANES:
            keep = row >= d
            a_sh = jnp.where(keep, pltpu.roll(a, d, axis=0), 1.0)
            h_sh = jnp.where(keep, pltpu.roll(h, d, axis=0), 0.0)
            h = a * h_sh + h
            a = a * a_sh
        else:
            h = jnp.concatenate([h[:d], a[d:] * h[:ts - d] + h[d:]], axis=0)
            a = jnp.concatenate([a[:d], a[d:] * a[:ts - d]], axis=0)
        if d >= SUBLANES // 2:
            products.extend(f() for f in itertools.islice(matmuls, 1))
        d *= 2
    kn, vt, kr_pre, gb, zcg = products + [f() for f in matmuls]
    h = h + a * h_carry[...]
    h_carry[...] = h[ts - 1:ts, :]
    ya_ref[...] = (h * _silu(ga)).astype(BF16)

    cos = cos_ref[...]
    shi = shi_ref[...]
    slo = slo_ref[...]
    kr = _rope(kr_pre, cos, shi, slo)
    for hd in range(MLA_HEADS):
        sl = slice(hd * HEAD_PAD, (hd + 1) * HEAD_PAD)
        q_ref[:, sl] = (_rope(qf[:, sl], cos, shi, slo) * Q_SCALE).astype(BF16)
        k_ref[:, sl] = (kn[:, sl] + kr).astype(BF16)
    tk = vt_ref.shape[-1]
    ones_rows = (lax.broadcasted_iota(jnp.int32, (VT_ROWS - V_HEAD_DIM, tk), 0) == 0).astype(BF16)
    for hd in range(MLA_HEADS):
        for c in range(ts // tk):
            vt_ref[hd, c, 0:V_HEAD_DIM, :] = (
                vt[hd * V_HEAD_DIM:(hd + 1) * V_HEAD_DIM, c * tk:(c + 1) * tk].astype(BF16))
            vt_ref[hd, c, V_HEAD_DIM:VT_ROWS, :] = ones_rows
    gbs_ref[...] = _silu(gb)

    zc = zcg[:, :D_POOL]
    gc = zcg[:, D_POOL:]
    zc_ext[POOL_HALO:POOL_HALO + ts, :] = zc
    lane = lax.broadcasted_iota(jnp.int32, (1, LANES), 1)
    pos1 = (t * ts + row + 1).astype(F32)
    pooled = []
    for blk in range(D_POOL // LANES):
        w_small, w_big = POOL_WINDOWS[2 * blk], POOL_WINDOWS[2 * blk + 1]
        cols = slice(blk * LANES, (blk + 1) * LANES)
        acc = zc_ext[:, cols]
        sums = {}
        w = 1
        while w < w_big:
            acc = acc + pltpu.roll(acc, w, axis=0)
            w *= 2
            sums[w] = acc[POOL_HALO:, :]
        first = lane < POOL_GROUP_DIM
        wsum = jnp.where(first, sums[w_small], sums[w_big])
        cnt = jnp.minimum(pos1, jnp.where(first, float(w_small), float(w_big)))
        pooled.append(wsum / cnt - zc[:, cols])
    zc_ext[0:POOL_HALO, :] = zc_ext[ts:ts + POOL_HALO, :]
    pooled = jnp.concatenate(pooled, axis=1).astype(BF16)
    yc_ref[...] = (_dot(pooled, wpool_ref[...]) * pscale_ref[...] * _silu(gc)).astype(BF16)


def _proj_call(x, lw, tabs, t_att):
    B, S, _ = x.shape
    ts = T_PROJ
    nt = S // ts
    row_blk = lambda w: pl.BlockSpec((None, ts, w), lambda b, t: (b, t, 0))
    full = lambda a: pl.BlockSpec(a.shape, lambda b, t: (0,) * a.ndim)
    tab = pl.BlockSpec((ts, LANES), lambda b, t: (t, 0))
    weights = [lw[n] for n in ("norm_g", "w_in", "conv_w", "conv_b", "w_gate", "b_gate", "lam",
                               "q_norm_g", "w_uq", "kv_norm_g", "w_uk", "w_vt", "w_pool", "pool_scale")]
    out_shape = (
        jax.ShapeDtypeStruct((B, S, D_LRU), BF16),
        jax.ShapeDtypeStruct((B, S, D_MLA), F32),
        jax.ShapeDtypeStruct((B, S, D_POOL), BF16),
        jax.ShapeDtypeStruct((B, S, D_QK), BF16),
        jax.ShapeDtypeStruct((B, S, D_QK), BF16),
        jax.ShapeDtypeStruct((B, MLA_HEADS, S // t_att, VT_ROWS, t_att), BF16),
    )
    out_specs = (
        row_blk(D_LRU), row_blk(D_MLA), row_blk(D_POOL), row_blk(D_QK), row_blk(D_QK),
        pl.BlockSpec((None, MLA_HEADS, ts // t_att, VT_ROWS, t_att), lambda b, t: (b, 0, t, 0, 0)),
    )
    return pl.pallas_call(
        _proj_kernel,
        grid=(B, nt),
        in_specs=[row_blk(D_MODEL)] + [full(w) for w in weights] + [tab, tab, tab],
        out_specs=out_specs,
        out_shape=out_shape,
        scratch_shapes=[
            pltpu.VMEM((ts + CONV_HALO, D_LRU), F32),
            pltpu.VMEM((ts + POOL_HALO, D_POOL), F32),
            pltpu.VMEM((1, D_LRU), F32),
        ],
        compiler_params=pltpu.CompilerParams(dimension_semantics=("arbitrary", "arbitrary")),
    )(x, *weights, *tabs)


def _attn_kernel(q_ref, k_ref, vt_ref, g_ref, o_ref, s_a, s_b):
    T = vt_ref.shape[-1]
    g = pl.program_id(2)
    n_c = T // Q_CHUNK
    chains = [(h, c) for h in range(2) for c in range(n_c)]
    n_chains = len(chains)

    def score_chain(r, j, n, nxt):
        h, c = chains[n]
        q = q_ref[r * T + c * Q_CHUNK:r * T + (c + 1) * Q_CHUNK, h * HEAD_PAD:(h + 1) * HEAD_PAD]
        s = _dot_nt(k_ref[pl.ds(pl.multiple_of(j * T, T), T), h * HEAD_PAD:(h + 1) * HEAD_PAD], q)
        nxt[h, :, c * Q_CHUNK:(c + 1) * Q_CHUNK] = s
        return jnp.max(s, axis=0, keepdims=True)

    def update_chain(j, n, cur, state, cmax, diag):
        h, c = chains[n]
        m, acc = state[n]
        rows = (c + 1) * Q_CHUNK if diag else T

        def load():
            s = cur[h, 0:rows, c * Q_CHUNK:(c + 1) * Q_CHUNK]
            if diag:
                kv_pos = lax.broadcasted_iota(jnp.int32, (rows, Q_CHUNK), 0)
                q_pos = lax.broadcasted_iota(jnp.int32, (rows, Q_CHUNK), 1) + c * Q_CHUNK
                s = jnp.where(kv_pos <= q_pos, s, NEG)
            return s

        tile_max = jnp.max(load(), axis=0, keepdims=True) if diag else cmax[n]
        m_new = jnp.maximum(m, tile_max)
        alpha = jnp.exp2(m - m_new)
        p = jnp.exp2(load() - m_new).astype(BF16)
        return m_new, alpha * acc + _dot(vt_ref[h, j, :, 0:rows], p)

    def stage(r, j, cur, nxt, carry):
        state, cmax = carry
        new_state, new_cmax = [], []
        for n in range(n_chains):
            new_cmax.append(score_chain(r, j + 1, n, nxt))
            new_state.append(update_chain(j, n, cur, state, cmax, False))
        return tuple(new_state), tuple(new_cmax)

    def finish(r, j, cur, carry, nxt):
        state, new_cmax = [], []
        for n in range(n_chains):
            if nxt is not None:
                new_cmax.append(score_chain(r + 1, 0, n, nxt))
            state.append(update_chain(j, n, cur, carry[0], None, True))
        outs = [acc[:V_HEAD_DIM] * (1.0 / acc[V_HEAD_DIM:V_HEAD_DIM + 1]) for _, acc in state]
        out_t = jnp.concatenate(
            [jnp.concatenate(outs[h * n_c:(h + 1) * n_c], axis=1) for h in range(2)], axis=0)
        o_ref[r * T:(r + 1) * T, :] = (out_t.T * g_ref[r * T:(r + 1) * T, :]).astype(BF16)
        return tuple(new_cmax)

    init = tuple((jnp.full((1, Q_CHUNK), NEG, F32), jnp.zeros((VT_ROWS, Q_CHUNK), F32))
                 for _ in chains)
    cmax = tuple(score_chain(0, 0, n, s_a) for n in range(n_chains))
    first, other = s_a, s_b
    for r in range(Q_TILES_PER_STEP):
        n_full = Q_TILES_PER_STEP * g + r
        x, y = first, other

        def pair(j, carry, r=r, x=x, y=y):
            return stage(r, j + 1, y, x, stage(r, j, x, y, carry))

        def quad(u, carry):
            return pair(4 * u + 2, pair(4 * u, carry))

        carry = lax.fori_loop(0, g, quad, (init, cmax))
        if r >= 2:
            carry = pair(n_full - r, carry)
        if r % 2:
            carry = stage(r, n_full - 1, x, y, carry)
            x, y = y, x
        last = r == Q_TILES_PER_STEP - 1
        cmax = finish(r, n_full, x, carry, None if last else y)
        first, other = y, x


def _attn_call(q, k, vt, gbs):
    B, S, _ = q.shape
    T = vt.shape[-1]
    nq = S // T
    rows = Q_TILES_PER_STEP * T
    pair = 2 * HEAD_PAD
    return pl.pallas_call(
        _attn_kernel,
        grid=(B, MLA_HEADS // 2, nq // Q_TILES_PER_STEP),
        in_specs=[
            pl.BlockSpec((None, rows, pair), lambda b, hp, g: (b, g, hp)),
            pl.BlockSpec((None, S, pair), lambda b, hp, g: (b, 0, hp)),
            pl.BlockSpec((None, 2, nq, VT_ROWS, T), lambda b, hp, g: (b, hp, 0, 0, 0)),
            pl.BlockSpec((None, rows, 2 * V_HEAD_DIM), lambda b, hp, g: (b, g, hp)),
        ],
        out_specs=pl.BlockSpec((None, rows, 2 * V_HEAD_DIM), lambda b, hp, g: (b, g, hp)),
        out_shape=jax.ShapeDtypeStruct((B, S, D_MLA), BF16),
        scratch_shapes=[pltpu.VMEM((2, T, T), F32), pltpu.VMEM((2, T, T), F32)],
        compiler_params=pltpu.CompilerParams(
            dimension_semantics=("arbitrary", "arbitrary", "arbitrary")),
    )(q, k, vt, gbs)


def _out_kernel(x_ref, ya_ref, yb_ref, yc_ref, w_ref, fg_ref, o_ref, *, final):
    y = (_dot(ya_ref[...], w_ref[0:D_LRU, :])
         + _dot(yb_ref[...], w_ref[D_LRU:D_LRU + D_MLA, :])
         + _dot(yc_ref[...], w_ref[D_LRU + D_MLA:D_MIX, :]))
    xn = x_ref[...] + y
    if final:
        xn = _rms(xn, fg_ref[...])
    o_ref[...] = xn


def _out_call(x, ya, yb, yc, w_out, final_g, final):
    B, S, _ = x.shape
    ts = T_OUT
    row_blk = lambda w: pl.BlockSpec((None, ts, w), lambda b, t: (b, t, 0))
    full = lambda a: pl.BlockSpec(a.shape, lambda b, t: (0,) * a.ndim)
    return pl.pallas_call(
        functools.partial(_out_kernel, final=final),
        grid=(B, S // ts),
        in_specs=[row_blk(D_MODEL), row_blk(D_LRU), row_blk(D_MLA), row_blk(D_POOL),
                  full(w_out), full(final_g)],
        out_specs=row_blk(D_MODEL),
        out_shape=jax.ShapeDtypeStruct((B, S, D_MODEL), F32),
        compiler_params=pltpu.CompilerParams(dimension_semantics=("arbitrary", "arbitrary")),
    )(x, ya, yb, yc, w_out, final_g)


def _block_diag(w):
    dp, g, n, _ = w.shape
    eye = jnp.eye(g, dtype=w.dtype)
    return jnp.einsum("dgij,gh->dgihj", w, eye).reshape(dp, g * n, g * n)


def _pack_weights(norm_g, w_in, conv_w, conv_b, w_rg, b_rg, w_ig, b_ig, lru_lambda,
                  q_norm_g, w_uq, kv_norm_g, w_ukv, w_pool, pool_scale, w_out):
    offs = np.cumsum((0,) + IN_SIZES)
    seg = lambda n: w_in[:, :, offs[n]:offs[n + 1]]
    kr_pad = jnp.pad(seg(4), ((0, 0), (0, 0), (ROPE_LO, HEAD_PAD - ROPE_LO - QK_ROPE_DIM)))
    w_in_p = jnp.concatenate([seg(0), seg(1), seg(2), seg(3), seg(5), seg(6), seg(7), kr_pad],
                             axis=-1).astype(BF16)
    w_gate = jnp.concatenate([_block_diag(w_rg), _block_diag(w_ig)], axis=-1).astype(BF16)
    b_gate = jnp.concatenate([b_rg, b_ig], axis=-1)[:, None, :]
    uq = w_uq.reshape(DEPTH, Q_LORA_RANK, MLA_HEADS, QK_NOPE_DIM + QK_ROPE_DIM)
    uq = jnp.pad(uq, ((0, 0), (0, 0), (0, 0), (0, HEAD_PAD - QK_NOPE_DIM - QK_ROPE_DIM)))
    ukv = w_ukv.reshape(DEPTH, KV_LORA_RANK, MLA_HEADS, QK_NOPE_DIM + V_HEAD_DIM)
    uk = jnp.pad(ukv[..., :QK_NOPE_DIM], ((0, 0), (0, 0), (0, 0), (0, HEAD_PAD - QK_NOPE_DIM)))
    uv = ukv[..., QK_NOPE_DIM:].reshape(DEPTH, KV_LORA_RANK, D_MLA)
    return {
        "norm_g": norm_g[:, None, :],
        "w_in": w_in_p,
        "conv_w": conv_w,
        "conv_b": conv_b[:, None, :],
        "w_gate": w_gate,
        "b_gate": b_gate,
        "lam": lru_lambda[:, None, :],
        "q_norm_g": q_norm_g[:, None, :],
        "w_uq": uq.reshape(DEPTH, Q_LORA_RANK, D_QK).astype(BF16),
        "kv_norm_g": kv_norm_g[:, None, :],
        "w_uk": uk.reshape(DEPTH, KV_LORA_RANK, D_QK).astype(BF16),
        "w_vt": jnp.swapaxes(uv, 1, 2).astype(BF16),
        "w_pool": _block_diag(w_pool).astype(BF16),
        "pool_scale": pool_scale[:, None, :],
        "w_out": w_out.astype(BF16),
    }


def _rope_tables(seq_len):
    pos = jnp.arange(seq_len, dtype=F32)
    inv_freq = ROPE_BASE ** (-jnp.arange(0, QK_ROPE_DIM, 2, dtype=F32) / QK_ROPE_DIM)
    ang = pos[:, None] * inv_freq[None, :]
    cos, sin = jnp.cos(ang), jnp.sin(ang)
    zeros = jnp.zeros_like(sin)
    tail = jnp.zeros((seq_len, HEAD_PAD - ROPE_LO - QK_ROPE_DIM), F32)
    ones = jnp.ones((seq_len, ROPE_LO), F32)
    lead = jnp.zeros((seq_len, ROPE_LO), F32)
    cos_t = jnp.concatenate([ones, cos, cos, tail], axis=1)
    sin_hi = jnp.concatenate([lead, -sin, zeros, tail], axis=1)
    sin_lo = jnp.concatenate([lead, zeros, sin, tail], axis=1)
    return cos_t, sin_hi, sin_lo


def kernel(x, norm_g, w_in, conv_w, conv_b, w_rg, b_rg, w_ig, b_ig, lru_lambda, q_norm_g, w_uq,
           kv_norm_g, w_ukv, w_pool, pool_scale, w_out, final_norm_g):
    B, S, D = x.shape
    assert D == D_MODEL and S % T_PROJ == 0 and S % T_ATT == 0 and T_PROJ % T_ATT == 0
    packed = _pack_weights(norm_g, w_in, conv_w, conv_b, w_rg, b_rg, w_ig, b_ig, lru_lambda,
                           q_norm_g, w_uq, kv_norm_g, w_ukv, w_pool, pool_scale, w_out)
    tabs = _rope_tables(S)
    final_g = final_norm_g[None, :]
    for l in range(DEPTH):
        lw = {n: v[l] for n, v in packed.items()}
        ya, gbs, yc, q, k, vt = _proj_call(x, lw, tabs, T_ATT)
        yb = _attn_call(q, k, vt, gbs)
        x = _out_call(x, ya, yb, yc, lw["w_out"], final_g, final=(l == DEPTH - 1))
    return x
```

```python
import functools
import itertools

import jax
import jax.numpy as jnp
import numpy as np
from jax import lax
from jax.experimental import pallas as pl
from jax.experimental.pallas import tpu as pltpu

F32 = jnp.float32
BF16 = jnp.bfloat16

D_MODEL = 1024
DEPTH = 4
EPS = 1e-6
D_LRU = 384
LRU_HEADS = 6
LRU_HEAD_DIM = 64
CONV_WIDTH = 4
LRU_C = 8.0
MLA_HEADS = 6
QK_NOPE_DIM = 64
QK_ROPE_DIM = 32
V_HEAD_DIM = 64
D_MLA = MLA_HEADS * V_HEAD_DIM
Q_LORA_RANK = 384
KV_LORA_RANK = 256
ROPE_BASE = 10000.0
POOL_WINDOWS = (2, 4, 8, 16)
POOL_GROUP_DIM = 64
D_POOL = POOL_GROUP_DIM * len(POOL_WINDOWS)
D_MIX = D_LRU + D_MLA + D_POOL
IN_SIZES = (D_LRU, D_LRU, Q_LORA_RANK, KV_LORA_RANK, QK_ROPE_DIM, D_MLA, D_POOL, D_POOL)

LANES = 128
SUBLANES = 8
HEAD_PAD = LANES
D_QK = MLA_HEADS * HEAD_PAD
ROPE_LO = QK_NOPE_DIM
HALF = QK_ROPE_DIM // 2
CONV_HALO = 8
POOL_HALO = 16
NEG = -1e30
BF16_SUBLANES = 16
VT_ROWS = V_HEAD_DIM + BF16_SUBLANES
Q_SCALE = (QK_NOPE_DIM + QK_ROPE_DIM) ** -0.5 * float(np.log2(np.e))

C_ZA = 0
C_GA = C_ZA + D_LRU
C_CQ = C_GA + D_LRU
C_KR = C_CQ + Q_LORA_RANK
C_CKV = C_KR + HEAD_PAD
C_GB = C_CKV + KV_LORA_RANK
C_ZC = C_GB + D_MLA
C_GC = C_ZC + D_POOL
D_IN_PAD = C_GC + D_POOL

T_PROJ = 512
T_ATT = 512
T_OUT = 1024
Q_CHUNK = 256
Q_TILES_PER_STEP = 4
assert Q_TILES_PER_STEP == 4


def _dot(a, b):
    return jnp.dot(a, b, preferred_element_type=F32)


def _dot_nt(a, b):
    return lax.dot_general(a, b, (((1,), (1,)), ((), ())), preferred_element_type=F32)


def _rms(x, g):
    return x * lax.rsqrt(jnp.mean(x * x, axis=-1, keepdims=True) + EPS) * g


def _silu(x):
    return x * jax.nn.sigmoid(x)


def _rope(x, cos, sin_hi, sin_lo):
    return (x * cos + pltpu.roll(x, LANES - HALF, axis=1) * sin_hi
            + pltpu.roll(x, HALF, axis=1) * sin_lo)


def _proj_kernel(x_ref, ng_ref, win_ref, convw_ref, convb_ref, wg_ref, bg_ref, lam_ref,
                 qg_ref, wuq_ref, kvg_ref, wuk_ref, wvt_ref, wpool_ref, pscale_ref,
                 cos_ref, shi_ref, slo_ref,
                 ya_ref, gbs_ref, yc_ref, q_ref, k_ref, vt_ref,
                 za_ext, zc_ext, h_carry):
    ts = x_ref.shape[0]
    t = pl.program_id(1)

    @pl.when(t == 0)
    def _():
        za_ext[0:CONV_HALO, :] = jnp.zeros((CONV_HALO, D_LRU), F32)
        zc_ext[0:POOL_HALO, :] = jnp.zeros((POOL_HALO, D_POOL), F32)
        h_carry[...] = jnp.zeros_like(h_carry)

    hb = _rms(x_ref[...], ng_ref[...]).astype(BF16)

    zag = _dot(hb, win_ref[:, C_ZA:C_CQ])
    za = zag[:, :D_LRU]
    ga = zag[:, D_LRU:]
    za_ext[CONV_HALO:CONV_HALO + ts, :] = za
    xa = convb_ref[...]
    for k in range(CONV_WIDTH):
        off = CONV_HALO - (CONV_WIDTH - 1) + k
        xa = xa + za_ext[off:off + ts, :] * convw_ref[k:k + 1, :]
    za_ext[0:CONV_HALO, :] = za_ext[ts:ts + CONV_HALO, :]
    gate_pre = _dot(xa.astype(BF16), wg_ref[...]) + bg_ref[...]

    cqk = _dot(hb, win_ref[:, C_CQ:C_CKV])
    cq = cqk[:, :Q_LORA_RANK]
    kr_pre = cqk[:, Q_LORA_RANK:]
    gates = jax.nn.sigmoid(gate_pre)
    r = gates[:, :D_LRU]
    i = gates[:, D_LRU:]
    nl = -lam_ref[...]
    softplus = jnp.maximum(nl, 0.0) + jnp.log1p(jnp.exp(-jnp.abs(nl)))
    log_a = (-LRU_C) * r * softplus
    a = jnp.exp(log_a)
    h = jnp.sqrt(-jnp.tanh(log_a) * (a * a + 1.0)) * (i * xa)

    ckv = _dot(hb, win_ref[:, C_CKV:C_GB])
    qf = _dot(_rms(cq, qg_ref[...]).astype(BF16), wuq_ref[...])
    kvn = _rms(ckv, kvg_ref[...]).astype(BF16)

    row = lax.broadcasted_iota(jnp.int32, (ts, 1), 0)
    matmuls = iter((
        lambda: _dot(kvn, wuk_ref[...]),
        lambda: _dot_nt(wvt_ref[...], kvn),
        lambda: _dot(hb, win_ref[:, C_GB:D_IN_PAD]),
    ))
    products = []
    d = 1
    while d < ts:
        if d < SUBLANES:
            keep = row >= d
            a_sh = jnp.where(keep, pltpu.roll(a, d, axis=0), 1.0)
            h_sh = jnp.where(keep, pltpu.roll(h, d, axis=0), 0.0)
            h = a * h_sh + h
            a = a * a_sh
        else:
            h = jnp.concatenate([h[:d], a[d:] * h[:ts - d] + h[d:]], axis=0)
            a = jnp.concatenate([a[:d], a[d:] * a[:ts - d]], axis=0)
        if d >= SUBLANES // 2:
            products.extend(f() for f in itertools.islice(matmuls, 1))
        d *= 2
    kn, vt, gbz = products + [f() for f in matmuls]
    gb = gbz[:, :C_ZC - C_GB]
    zc = gbz[:, C_ZC - C_GB:C_GC - C_GB]
    gc = gbz[:, C_GC - C_GB:]
    h = h + a * h_carry[...]
    h_carry[...] = h[ts - 1:ts, :]
    ya_ref[...] = (h * _silu(ga)).astype(BF16)

    cos = cos_ref[...]
    shi = shi_ref[...]
    slo = slo_ref[...]
    kr = _rope(kr_pre, cos, shi, slo)
    for hd in range(MLA_HEADS):
        sl = slice(hd * HEAD_PAD, (hd + 1) * HEAD_PAD)
        q_ref[:, sl] = (_rope(qf[:, sl], cos, shi, slo) * Q_SCALE).astype(BF16)
        k_ref[:, sl] = (kn[:, sl] + kr).astype(BF16)
    tk = vt_ref.shape[-1]
    ones_rows = (lax.broadcasted_iota(jnp.int32, (VT_ROWS - V_HEAD_DIM, tk), 0) == 0).astype(BF16)
    for hd in range(MLA_HEADS):
        for c in range(ts // tk):
            vt_ref[hd, c, 0:V_HEAD_DIM, :] = (
                vt[hd * V_HEAD_DIM:(hd + 1) * V_HEAD_DIM, c * tk:(c + 1) * tk].astype(BF16))
            vt_ref[hd, c, V_HEAD_DIM:VT_ROWS, :] = ones_rows
    gbs_ref[...] = _silu(gb)

    zc_ext[POOL_HALO:POOL_HALO + ts, :] = zc
    lane = lax.broadcasted_iota(jnp.int32, (1, LANES), 1)
    pos1 = (t * ts + row + 1).astype(F32)
    pooled = []
    for blk in range(D_POOL // LANES):
        w_small, w_big = POOL_WINDOWS[2 * blk], POOL_WINDOWS[2 * blk + 1]
        cols = slice(blk * LANES, (blk + 1) * LANES)
        acc = zc_ext[:, cols]
        sums = {}
        w = 1
        while w < w_big:
            acc = acc + pltpu.roll(acc, w, axis=0)
            w *= 2
            sums[w] = acc[POOL_HALO:, :]
        first = lane < POOL_GROUP_DIM
        wsum = jnp.where(first, sums[w_small], sums[w_big])
        cnt = jnp.minimum(pos1, jnp.where(first, float(w_small), float(w_big)))
        pooled.append(wsum / cnt - zc[:, cols])
    zc_ext[0:POOL_HALO, :] = zc_ext[ts:ts + POOL_HALO, :]
    pooled = jnp.concatenate(pooled, axis=1).astype(BF16)
    yc_ref[...] = (_dot(pooled, wpool_ref[...]) * pscale_ref[...] * _silu(gc)).astype(BF16)


def _layer_spec(a, layer):
    return pl.BlockSpec((None,) + a.shape[1:], lambda b, t: (layer,) + (0,) * (a.ndim - 1))


def _proj_call(x, packed, layer, tabs, t_att):
    B, S, _ = x.shape
    ts = T_PROJ
    nt = S // ts
    row_blk = lambda w: pl.BlockSpec((None, ts, w), lambda b, t: (b, t, 0))
    full = lambda a: _layer_spec(a, layer)
    tab = pl.BlockSpec((ts, LANES), lambda b, t: (t, 0))
    weights = [packed[n] for n in ("norm_g", "w_in", "conv_w", "conv_b", "w_gate", "b_gate", "lam",
                                   "q_norm_g", "w_uq", "kv_norm_g", "w_uk", "w_vt", "w_pool", "pool_scale")]
    out_shape = (
        jax.ShapeDtypeStruct((B, S, D_LRU), BF16),
        jax.ShapeDtypeStruct((B, S, D_MLA), F32),
        jax.ShapeDtypeStruct((B, S, D_POOL), BF16),
        jax.ShapeDtypeStruct((B, S, D_QK), BF16),
        jax.ShapeDtypeStruct((B, S, D_QK), BF16),
        jax.ShapeDtypeStruct((B, MLA_HEADS, S // t_att, VT_ROWS, t_att), BF16),
    )
    out_specs = (
        row_blk(D_LRU), row_blk(D_MLA), row_blk(D_POOL), row_blk(D_QK), row_blk(D_QK),
        pl.BlockSpec((None, MLA_HEADS, ts // t_att, VT_ROWS, t_att), lambda b, t: (b, 0, t, 0, 0)),
    )
    return pl.pallas_call(
        _proj_kernel,
        grid=(B, nt),
        in_specs=[row_blk(D_MODEL)] + [full(w) for w in weights] + [tab, tab, tab],
        out_specs=out_specs,
        out_shape=out_shape,
        scratch_shapes=[
            pltpu.VMEM((ts + CONV_HALO, D_LRU), F32),
            pltpu.VMEM((ts + POOL_HALO, D_POOL), F32),
            pltpu.VMEM((1, D_LRU), F32),
        ],
        compiler_params=pltpu.CompilerParams(dimension_semantics=("arbitrary", "arbitrary")),
    )(x, *weights, *tabs)


def _attn_kernel(q_ref, k_ref, vt_ref, g_ref, o_ref, s_a, s_b):
    T = vt_ref.shape[-1]
    g = pl.program_id(2)
    n_c = T // Q_CHUNK
    chains = [(h, c) for h in range(2) for c in range(n_c)]
    n_chains = len(chains)

    def score_chain(r, j, n, nxt):
        h, c = chains[n]
        q = q_ref[r * T + c * Q_CHUNK:r * T + (c + 1) * Q_CHUNK, h * HEAD_PAD:(h + 1) * HEAD_PAD]
        s = _dot_nt(k_ref[pl.ds(pl.multiple_of(j * T, T), T), h * HEAD_PAD:(h + 1) * HEAD_PAD], q)
        nxt[h, :, c * Q_CHUNK:(c + 1) * Q_CHUNK] = s
        return jnp.max(s, axis=0, keepdims=True)

    def update_chain(j, n, cur, state, cmax, diag):
        h, c = chains[n]
        m, acc = state[n]
        rows = (c + 1) * Q_CHUNK if diag else T

        def load():
            s = cur[h, 0:rows, c * Q_CHUNK:(c + 1) * Q_CHUNK]
            if diag:
                kv_pos = lax.broadcasted_iota(jnp.int32, (rows, Q_CHUNK), 0)
                q_pos = lax.broadcasted_iota(jnp.int32, (rows, Q_CHUNK), 1) + c * Q_CHUNK
                s = jnp.where(kv_pos <= q_pos, s, NEG)
            return s

        tile_max = jnp.max(load(), axis=0, keepdims=True) if diag else cmax[n]
        m_new = jnp.maximum(m, tile_max)
        alpha = jnp.exp2(m - m_new)
        p = jnp.exp2(load() - m_new).astype(BF16)
        return m_new, alpha * acc + _dot(vt_ref[h, j, :, 0:rows], p)

    def stage(r, j, cur, nxt, carry):
        state, cmax = carry
        new_state, new_cmax = [], []
        for n in range(n_chains):
            new_cmax.append(score_chain(r, j + 1, n, nxt))
            new_state.append(update_chain(j, n, cur, state, cmax, False))
        return tuple(new_state), tuple(new_cmax)

    def finish(r, j, cur, carry, nxt):
        state, new_cmax = [], []
        for n in range(n_chains):
            if nxt is not None:
                new_cmax.append(score_chain(r + 1, 0, n, nxt))
            state.append(update_chain(j, n, cur, carry[0], None, True))
        outs = [acc[:V_HEAD_DIM] * (1.0 / acc[V_HEAD_DIM:V_HEAD_DIM + 1]) for _, acc in state]
        out_t = jnp.concatenate(
            [jnp.concatenate(outs[h * n_c:(h + 1) * n_c], axis=1) for h in range(2)], axis=0)
        o_ref[r * T:(r + 1) * T, :] = (out_t.T * g_ref[r * T:(r + 1) * T, :]).astype(BF16)
        return tuple(new_cmax)

    init = tuple((jnp.full((1, Q_CHUNK), NEG, F32), jnp.zeros((VT_ROWS, Q_CHUNK), F32))
                 for _ in chains)
    cmax = tuple(score_chain(0, 0, n, s_a) for n in range(n_chains))
    first, other = s_a, s_b
    for r in range(Q_TILES_PER_STEP):
        n_full = Q_TILES_PER_STEP * g + r
        x, y = first, other

        def pair(j, carry, r=r, x=x, y=y):
            return stage(r, j + 1, y, x, stage(r, j, x, y, carry))

        def quad(u, carry):
            return pair(4 * u + 2, pair(4 * u, carry))

        carry = lax.fori_loop(0, g, quad, (init, cmax))
        if r >= 2:
            carry = pair(n_full - r, carry)
        if r % 2:
            carry = stage(r, n_full - 1, x, y, carry)
            x, y = y, x
        last = r == Q_TILES_PER_STEP - 1
        cmax = finish(r, n_full, x, carry, None if last else y)
        first, other = y, x


def _attn_call(q, k, vt, gbs):
    B, S, _ = q.shape
    T = vt.shape[-1]
    nq = S // T
    rows = Q_TILES_PER_STEP * T
    pair = 2 * HEAD_PAD
    return pl.pallas_call(
        _attn_kernel,
        grid=(B, MLA_HEADS // 2, nq // Q_TILES_PER_STEP),
        in_specs=[
            pl.BlockSpec((None, rows, pair), lambda b, hp, g: (b, g, hp)),
            pl.BlockSpec((None, S, pair), lambda b, hp, g: (b, 0, hp)),
            pl.BlockSpec((None, 2, nq, VT_ROWS, T), lambda b, hp, g: (b, hp, 0, 0, 0)),
            pl.BlockSpec((None, rows, 2 * V_HEAD_DIM), lambda b, hp, g: (b, g, hp)),
        ],
        out_specs=pl.BlockSpec((None, rows, 2 * V_HEAD_DIM), lambda b, hp, g: (b, g, hp)),
        out_shape=jax.ShapeDtypeStruct((B, S, D_MLA), BF16),
        scratch_shapes=[pltpu.VMEM((2, T, T), F32), pltpu.VMEM((2, T, T), F32)],
        compiler_params=pltpu.CompilerParams(
            dimension_semantics=("arbitrary", "arbitrary", "arbitrary")),
    )(q, k, vt, gbs)


def _out_kernel(x_ref, ya_ref, yb_ref, yc_ref, w_ref, fg_ref, o_ref, *, final):
    y = jnp.concatenate([ya_ref[...], yb_ref[...], yc_ref[...]], axis=1)
    xn = x_ref[...] + _dot(y, w_ref[...])
    if final:
        xn = _rms(xn, fg_ref[...])
    o_ref[...] = xn


def _out_call(x, ya, yb, yc, w_out, layer, final_g, final):
    B, S, _ = x.shape
    ts = T_OUT
    row_blk = lambda w: pl.BlockSpec((None, ts, w), lambda b, t: (b, t, 0))
    return pl.pallas_call(
        functools.partial(_out_kernel, final=final),
        grid=(B, S // ts),
        in_specs=[row_blk(D_MODEL), row_blk(D_LRU), row_blk(D_MLA), row_blk(D_POOL),
                  _layer_spec(w_out, layer), pl.BlockSpec(final_g.shape, lambda b, t: (0, 0))],
        out_specs=row_blk(D_MODEL),
        out_shape=jax.ShapeDtypeStruct((B, S, D_MODEL), F32),
        compiler_params=pltpu.CompilerParams(dimension_semantics=("arbitrary", "arbitrary")),
    )(x, ya, yb, yc, w_out, final_g)


def _block_diag(w):
    dp, g, n, _ = w.shape
    eye = jnp.eye(g, dtype=w.dtype)
    return jnp.einsum("dgij,gh->dgihj", w, eye).reshape(dp, g * n, g * n)


def _pack_weights(norm_g, w_in, conv_w, conv_b, w_rg, b_rg, w_ig, b_ig, lru_lambda,
                  q_norm_g, w_uq, kv_norm_g, w_ukv, w_pool, pool_scale, w_out):
    offs = np.cumsum((0,) + IN_SIZES)
    seg = lambda n: w_in[:, :, offs[n]:offs[n + 1]]
    kr_pad = jnp.pad(seg(4), ((0, 0), (0, 0), (ROPE_LO, HEAD_PAD - ROPE_LO - QK_ROPE_DIM)))
    w_in_p = jnp.concatenate([seg(0), seg(1), seg(2), kr_pad, seg(3), seg(5), seg(6), seg(7)],
                             axis=-1).astype(BF16)
    w_gate = jnp.concatenate([_block_diag(w_rg), _block_diag(w_ig)], axis=-1).astype(BF16)
    b_gate = jnp.concatenate([b_rg, b_ig], axis=-1)[:, None, :]
    uq = w_uq.reshape(DEPTH, Q_LORA_RANK, MLA_HEADS, QK_NOPE_DIM + QK_ROPE_DIM)
    uq = jnp.pad(uq, ((0, 0), (0, 0), (0, 0), (0, HEAD_PAD - QK_NOPE_DIM - QK_ROPE_DIM)))
    ukv = w_ukv.reshape(DEPTH, KV_LORA_RANK, MLA_HEADS, QK_NOPE_DIM + V_HEAD_DIM)
    uk = jnp.pad(ukv[..., :QK_NOPE_DIM], ((0, 0), (0, 0), (0, 0), (0, HEAD_PAD - QK_NOPE_DIM)))
    uv = ukv[..., QK_NOPE_DIM:].reshape(DEPTH, KV_LORA_RANK, D_MLA)
    return {
        "norm_g": norm_g[:, None, :],
        "w_in": w_in_p,
        "conv_w": conv_w,
        "conv_b": conv_b[:, None, :],
        "w_gate": w_gate,
        "b_gate": b_gate,
        "lam": lru_lambda[:, None, :],
        "q_norm_g": q_norm_g[:, None, :],
        "w_uq": uq.reshape(DEPTH, Q_LORA_RANK, D_QK).astype(BF16),
        "kv_norm_g": kv_norm_g[:, None, :],
        "w_uk": uk.reshape(DEPTH, KV_LORA_RANK, D_QK).astype(BF16),
        "w_vt": jnp.swapaxes(uv, 1, 2).astype(BF16),
        "w_pool": _block_diag(w_pool).astype(BF16),
        "pool_scale": pool_scale[:, None, :],
        "w_out": w_out.astype(BF16),
    }


def _rope_tables(seq_len):
    pos = jnp.arange(seq_len, dtype=F32)
    inv_freq = ROPE_BASE ** (-jnp.arange(0, QK_ROPE_DIM, 2, dtype=F32) / QK_ROPE_DIM)
    ang = pos[:, None] * inv_freq[None, :]
    cos, sin = jnp.cos(ang), jnp.sin(ang)
    zeros = jnp.zeros_like(sin)
    tail = jnp.zeros((seq_len, HEAD_PAD - ROPE_LO - QK_ROPE_DIM), F32)
    ones = jnp.ones((seq_len, ROPE_LO), F32)
    lead = jnp.zeros((seq_len, ROPE_LO), F32)
    cos_t = jnp.concatenate([ones, cos, cos, tail], axis=1)
    sin_hi = jnp.concatenate([lead, -sin, zeros, tail], axis=1)
    sin_lo = jnp.concatenate([lead, zeros, sin, tail], axis=1)
    return cos_t, sin_hi, sin_lo


def kernel(x, norm_g, w_in, conv_w, conv_b, w_rg, b_rg, w_ig, b_ig, lru_lambda, q_norm_g, w_uq,
           kv_norm_g, w_ukv, w_pool, pool_scale, w_out, final_norm_g):
    B, S, D = x.shape
    assert D == D_MODEL and S % T_PROJ == 0 and S % T_ATT == 0 and T_PROJ % T_ATT == 0
    packed = _pack_weights(norm_g, w_in, conv_w, conv_b, w_rg, b_rg, w_ig, b_ig, lru_lambda,
                           q_norm_g, w_uq, kv_norm_g, w_ukv, w_pool, pool_scale, w_out)
    tabs = _rope_tables(S)
    final_g = final_norm_g[None, :]
    for l in range(DEPTH):
        ya, gbs, yc, q, k, vt = _proj_call(x, packed, l, tabs, T_ATT)
        yb = _attn_call(q, k, vt, gbs)
        x = _out_call(x, ya, yb, yc, packed["w_out"], l, final_g, final=(l == DEPTH - 1))
    return x
```

```python
import functools
import itertools

import jax
import jax.numpy as jnp
import numpy as np
from jax import lax
from jax.experimental import pallas as pl
from jax.experimental.pallas import tpu as pltpu

F32 = jnp.float32
BF16 = jnp.bfloat16

D_MODEL = 1024
DEPTH = 4
EPS = 1e-6
D_LRU = 384
LRU_HEADS = 6
LRU_HEAD_DIM = 64
CONV_WIDTH = 4
LRU_C = 8.0
MLA_HEADS = 6
QK_NOPE_DIM = 64
QK_ROPE_DIM = 32
V_HEAD_DIM = 64
D_MLA = MLA_HEADS * V_HEAD_DIM
Q_LORA_RANK = 384
KV_LORA_RANK = 256
ROPE_BASE = 10000.0
POOL_WINDOWS = (2, 4, 8, 16)
POOL_GROUP_DIM = 64
D_POOL = POOL_GROUP_DIM * len(POOL_WINDOWS)
D_MIX = D_LRU + D_MLA + D_POOL
IN_SIZES = (D_LRU, D_LRU, Q_LORA_RANK, KV_LORA_RANK, QK_ROPE_DIM, D_MLA, D_POOL, D_POOL)

LANES = 128
SUBLANES = 8
HEAD_PAD = LANES
D_QK = MLA_HEADS * HEAD_PAD
ROPE_LO = QK_NOPE_DIM
HALF = QK_ROPE_DIM // 2
CONV_HALO = 8
POOL_HALO = 16
NEG = -1e30
BF16_SUBLANES = 16
VT_ROWS = V_HEAD_DIM + BF16_SUBLANES
Q_SCALE = (QK_NOPE_DIM + QK_ROPE_DIM) ** -0.5 * float(np.log2(np.e))

C_ZA = 0
C_GA = C_ZA + D_LRU
C_CQ = C_GA + D_LRU
C_CKV = C_CQ + Q_LORA_RANK
C_KR = C_CKV + KV_LORA_RANK
C_GB = C_KR + HEAD_PAD
C_ZC = C_GB + D_MLA
C_GC = C_ZC + D_POOL
D_IN_PAD = C_GC + D_POOL

T_PROJ = 512
T_ATT = 512
T_OUT = 1024
Q_CHUNK = 256
Q_TILES_PER_STEP = 4
assert Q_TILES_PER_STEP == 4


def _dot(a, b):
    return jnp.dot(a, b, preferred_element_type=F32)


def _dot_nt(a, b):
    return lax.dot_general(a, b, (((1,), (1,)), ((), ())), preferred_element_type=F32)


def _rms(x, g):
    return x * lax.rsqrt(jnp.mean(x * x, axis=-1, keepdims=True) + EPS) * g


def _silu(x):
    return x * jax.nn.sigmoid(x)


def _rope(x, cos, sin_hi, sin_lo):
    return (x * cos + pltpu.roll(x, LANES - HALF, axis=1) * sin_hi
            + pltpu.roll(x, HALF, axis=1) * sin_lo)


N_PROJ_PARAMS = 17


def _proj_kernel(x_ref, *refs):
    _proj_body(x_ref[...], *refs)


def _proj_fused_kernel(x_ref, ya_prev_ref, yb_prev_ref, yc_prev_ref, wout_ref, *refs):
    y = jnp.concatenate([ya_prev_ref[...], yb_prev_ref[...], yc_prev_ref[...]], axis=1)
    x = x_ref[...] + _dot(y, wout_ref[...])
    xo_ref = refs[N_PROJ_PARAMS]
    xo_ref[...] = x
    _proj_body(x, *refs[:N_PROJ_PARAMS], *refs[N_PROJ_PARAMS + 1:])


def _proj_body(x, ng_ref, win_ref, convw_ref, convb_ref, wg_ref, bg_ref, lam_ref,
               qg_ref, wuq_ref, kvg_ref, wuk_ref, wvt_ref, wpool_ref, pscale_ref,
               cos_ref, shi_ref, slo_ref,
               ya_ref, gbs_ref, yc_ref, q_ref, k_ref, vt_ref,
               za_ext, zc_ext, h_carry):
    ts = x.shape[0]
    t = pl.program_id(1)

    @pl.when(t == 0)
    def _():
        za_ext[0:CONV_HALO, :] = jnp.zeros((CONV_HALO, D_LRU), F32)
        zc_ext[0:POOL_HALO, :] = jnp.zeros((POOL_HALO, D_POOL), F32)
        h_carry[...] = jnp.zeros_like(h_carry)

    hb = _rms(x, ng_ref[...]).astype(BF16)

    zag = _dot(hb, win_ref[:, C_ZA:C_CQ])
    za = zag[:, :D_LRU]
    ga = zag[:, D_LRU:]
    za_ext[CONV_HALO:CONV_HALO + ts, :] = za
    xa = convb_ref[...]
    for k in range(CONV_WIDTH):
        off = CONV_HALO - (CONV_WIDTH - 1) + k
        xa = xa + za_ext[off:off + ts, :] * convw_ref[k:k + 1, :]
    za_ext[0:CONV_HALO, :] = za_ext[ts:ts + CONV_HALO, :]
    gate_pre = _dot(xa.astype(BF16), wg_ref[...]) + bg_ref[...]

    lat = _dot(hb, win_ref[:, C_CQ:C_GB])
    cq = lat[:, :C_CKV - C_CQ]
    ckv = lat[:, C_CKV - C_CQ:C_KR - C_CQ]
    kr_pre = lat[:, C_KR - C_CQ:]
    gates = jax.nn.sigmoid(gate_pre)
    r = gates[:, :D_LRU]
    i = gates[:, D_LRU:]
    nl = -lam_ref[...]
    softplus = jnp.maximum(nl, 0.0) + jnp.log1p(jnp.exp(-jnp.abs(nl)))
    log_a = (-LRU_C) * r * softplus
    a = jnp.exp(log_a)
    h = jnp.sqrt(-jnp.tanh(log_a) * (a * a + 1.0)) * (i * xa)

    qf = _dot(_rms(cq, qg_ref[...]).astype(BF16), wuq_ref[...])
    kvn = _rms(ckv, kvg_ref[...]).astype(BF16)

    row = lax.broadcasted_iota(jnp.int32, (ts, 1), 0)
    matmuls = iter((
        lambda: _dot(kvn, wuk_ref[...]),
        lambda: _dot_nt(wvt_ref[...], kvn),
        lambda: _dot(hb, win_ref[:, C_GB:D_IN_PAD]),
    ))
    products = []
    d = 1
    while d < ts:
        if d < SUBLANES:
            keep = row >= d
            a_sh = jnp.where(keep, pltpu.roll(a, d, axis=0), 1.0)
            h_sh = jnp.where(keep, pltpu.roll(h, d, axis=0), 0.0)
            h = a * h_sh + h
            a = a * a_sh
        else:
            h = jnp.concatenate([h[:d], a[d:] * h[:ts - d] + h[d:]], axis=0)
            a = jnp.concatenate([a[:d], a[d:] * a[:ts - d]], axis=0)
        if d >= SUBLANES // 2:
            products.extend(f() for f in itertools.islice(matmuls, 1))
        d *= 2
    kn, vt, gbz = products + [f() for f in matmuls]
    gb = gbz[:, :C_ZC - C_GB]
    zc = gbz[:, C_ZC - C_GB:C_GC - C_GB]
    gc = gbz[:, C_GC - C_GB:]
    h = h + a * h_carry[...]
    h_carry[...] = h[ts - 1:ts, :]
    ya_ref[...] = (h * _silu(ga)).astype(BF16)

    cos = cos_ref[...]
    shi = shi_ref[...]
    slo = slo_ref[...]
    kr = _rope(kr_pre, cos, shi, slo)
    for hd in range(MLA_HEADS):
        sl = slice(hd * HEAD_PAD, (hd + 1) * HEAD_PAD)
        q_ref[:, sl] = (_rope(qf[:, sl], cos, shi, slo) * Q_SCALE).astype(BF16)
        k_ref[:, sl] = (kn[:, sl] + kr).astype(BF16)
    tk = vt_ref.shape[-1]
    ones_rows = (lax.broadcasted_iota(jnp.int32, (VT_ROWS - V_HEAD_DIM, tk), 0) == 0).astype(BF16)
    for hd in range(MLA_HEADS):
        for c in range(ts // tk):
            vt_ref[hd, c, 0:V_HEAD_DIM, :] = (
                vt[hd * V_HEAD_DIM:(hd + 1) * V_HEAD_DIM, c * tk:(c + 1) * tk].astype(BF16))
            vt_ref[hd, c, V_HEAD_DIM:VT_ROWS, :] = ones_rows
    gbs_ref[...] = _silu(gb)

    zc_ext[POOL_HALO:POOL_HALO + ts, :] = zc
    lane = lax.broadcasted_iota(jnp.int32, (1, LANES), 1)
    pos1 = (t * ts + row + 1).astype(F32)
    pooled = []
    for blk in range(D_POOL // LANES):
        w_small, w_big = POOL_WINDOWS[2 * blk], POOL_WINDOWS[2 * blk + 1]
        cols = slice(blk * LANES, (blk + 1) * LANES)
        acc = zc_ext[:, cols]
        sums = {}
        w = 1
        while w < w_big:
            acc = acc + pltpu.roll(acc, w, axis=0)
            w *= 2
            sums[w] = acc[POOL_HALO:, :]
        first = lane < POOL_GROUP_DIM
        wsum = jnp.where(first, sums[w_small], sums[w_big])
        cnt = jnp.minimum(pos1, jnp.where(first, float(w_small), float(w_big)))
        pooled.append(wsum / cnt - zc[:, cols])
    zc_ext[0:POOL_HALO, :] = zc_ext[ts:ts + POOL_HALO, :]
    pooled = jnp.concatenate(pooled, axis=1).astype(BF16)
    yc_ref[...] = (_dot(pooled, wpool_ref[...]) * pscale_ref[...] * _silu(gc)).astype(BF16)


def _layer_spec(a, layer):
    return pl.BlockSpec((None,) + a.shape[1:], lambda b, t: (layer,) + (0,) * (a.ndim - 1))


def _proj_call(x, packed, layer, tabs, t_att, prev=None):
    B, S, _ = x.shape
    ts = T_PROJ
    nt = S // ts
    row_blk = lambda w: pl.BlockSpec((None, ts, w), lambda b, t: (b, t, 0))
    full = lambda a: _layer_spec(a, layer)
    tab = pl.BlockSpec((ts, LANES), lambda b, t: (t, 0))
    weights = [packed[n] for n in ("norm_g", "w_in", "conv_w", "conv_b", "w_gate", "b_gate", "lam",
                                   "q_norm_g", "w_uq", "kv_norm_g", "w_uk", "w_vt", "w_pool", "pool_scale")]
    out_shape = (
        jax.ShapeDtypeStruct((B, S, D_LRU), BF16),
        jax.ShapeDtypeStruct((B, S, D_MLA), F32),
        jax.ShapeDtypeStruct((B, S, D_POOL), BF16),
        jax.ShapeDtypeStruct((B, S, D_QK), BF16),
        jax.ShapeDtypeStruct((B, S, D_QK), BF16),
        jax.ShapeDtypeStruct((B, MLA_HEADS, S // t_att, VT_ROWS, t_att), BF16),
    )
    out_specs = (
        row_blk(D_LRU), row_blk(D_MLA), row_blk(D_POOL), row_blk(D_QK), row_blk(D_QK),
        pl.BlockSpec((None, MLA_HEADS, ts // t_att, VT_ROWS, t_att), lambda b, t: (b, 0, t, 0, 0)),
    )
    assert len(weights) + len(tabs) == N_PROJ_PARAMS
    param_specs = [full(w) for w in weights] + [tab, tab, tab]
    if prev is None:
        body, lead, lead_specs = _proj_kernel, (x,), [row_blk(D_MODEL)]
    else:
        body = _proj_fused_kernel
        lead = (x, *prev, packed["w_out"])
        lead_specs = [row_blk(D_MODEL), row_blk(D_LRU), row_blk(D_MLA), row_blk(D_POOL),
                      _layer_spec(packed["w_out"], layer - 1)]
        out_shape = (jax.ShapeDtypeStruct((B, S, D_MODEL), F32),) + out_shape
        out_specs = (row_blk(D_MODEL),) + out_specs
    return pl.pallas_call(
        body,
        grid=(B, nt),
        in_specs=lead_specs + param_specs,
        out_specs=out_specs,
        out_shape=out_shape,
        scratch_shapes=[
            pltpu.VMEM((ts + CONV_HALO, D_LRU), F32),
            pltpu.VMEM((ts + POOL_HALO, D_POOL), F32),
            pltpu.VMEM((1, D_LRU), F32),
        ],
        compiler_params=pltpu.CompilerParams(dimension_semantics=("arbitrary", "arbitrary")),
    )(*lead, *weights, *tabs)


def _attn_kernel(q_ref, k_ref, vt_ref, g_ref, o_ref, s_a, s_b):
    T = vt_ref.shape[-1]
    g = pl.program_id(2)
    n_c = T // Q_CHUNK
    chains = [(h, c) for h in range(2) for c in range(n_c)]
    n_chains = len(chains)

    def score_chain(r, j, n, nxt):
        h, c = chains[n]
        q = q_ref[r * T + c * Q_CHUNK:r * T + (c + 1) * Q_CHUNK, h * HEAD_PAD:(h + 1) * HEAD_PAD]
        s = _dot_nt(k_ref[pl.ds(pl.multiple_of(j * T, T), T), h * HEAD_PAD:(h + 1) * HEAD_PAD], q)
        nxt[h, :, c * Q_CHUNK:(c + 1) * Q_CHUNK] = s
        return jnp.max(s, axis=0, keepdims=True)

    def update_chain(j, n, cur, state, cmax, diag):
        h, c = chains[n]
        m, acc = state[n]
        rows = (c + 1) * Q_CHUNK if diag else T

        def load():
            s = cur[h, 0:rows, c * Q_CHUNK:(c + 1) * Q_CHUNK]
            if diag:
                kv_pos = lax.broadcasted_iota(jnp.int32, (rows, Q_CHUNK), 0)
                q_pos = lax.broadcasted_iota(jnp.int32, (rows, Q_CHUNK), 1) + c * Q_CHUNK
                s = jnp.where(kv_pos <= q_pos, s, NEG)
            return s

        tile_max = jnp.max(load(), axis=0, keepdims=True) if diag else cmax[n]
        m_new = jnp.maximum(m, tile_max)
        alpha = jnp.exp2(m - m_new)
        p = jnp.exp2(load() - m_new).astype(BF16)
        return m_new, alpha * acc + _dot(vt_ref[h, j, :, 0:rows], p)

    def stage(r, j, cur, nxt, carry):
        state, cmax = carry
        new_state, new_cmax = [], []
        for n in range(n_chains):
            new_cmax.append(score_chain(r, j + 1, n, nxt))
            new_state.append(update_chain(j, n, cur, state, cmax, False))
        return tuple(new_state), tuple(new_cmax)

    def finish(r, j, cur, carry, nxt):
        state, new_cmax = [], []
        for n in range(n_chains):
            if nxt is not None:
                new_cmax.append(score_chain(r + 1, 0, n, nxt))
            state.append(update_chain(j, n, cur, carry[0], None, True))
        outs = [acc[:V_HEAD_DIM] * (1.0 / acc[V_HEAD_DIM:V_HEAD_DIM + 1]) for _, acc in state]
        out_t = jnp.concatenate(
            [jnp.concatenate(outs[h * n_c:(h + 1) * n_c], axis=1) for h in range(2)], axis=0)
        o_ref[r * T:(r + 1) * T, :] = (out_t.T * g_ref[r * T:(r + 1) * T, :]).astype(BF16)
        return tuple(new_cmax)

    init = tuple((jnp.full((1, Q_CHUNK), NEG, F32), jnp.zeros((VT_ROWS, Q_CHUNK), F32))
                 for _ in chains)
    cmax = tuple(score_chain(0, 0, n, s_a) for n in range(n_chains))
    first, other = s_a, s_b
    for r in range(Q_TILES_PER_STEP):
        n_full = Q_TILES_PER_STEP * g + r
        x, y = first, other

        def pair(j, carry, r=r, x=x, y=y):
            return stage(r, j + 1, y, x, stage(r, j, x, y, carry))

        def quad(u, carry):
            return pair(4 * u + 2, pair(4 * u, carry))

        carry = lax.fori_loop(0, g, quad, (init, cmax))
        if r >= 2:
            carry = pair(n_full - r, carry)
        if r % 2:
            carry = stage(r, n_full - 1, x, y, carry)
            x, y = y, x
        last = r == Q_TILES_PER_STEP - 1
        cmax = finish(r, n_full, x, carry, None if last else y)
        first, other = y, x


def _attn_call(q, k, vt, gbs):
    B, S, _ = q.shape
    T = vt.shape[-1]
    nq = S // T
    rows = Q_TILES_PER_STEP * T
    pair = 2 * HEAD_PAD
    return pl.pallas_call(
        _attn_kernel,
        grid=(B, MLA_HEADS // 2, nq // Q_TILES_PER_STEP),
        in_specs=[
            pl.BlockSpec((None, rows, pair), lambda b, hp, g: (b, g, hp)),
            pl.BlockSpec((None, S, pair), lambda b, hp, g: (b, 0, hp)),
            pl.BlockSpec((None, 2, nq, VT_ROWS, T), lambda b, hp, g: (b, hp, 0, 0, 0)),
            pl.BlockSpec((None, rows, 2 * V_HEAD_DIM), lambda b, hp, g: (b, g, hp)),
        ],
        out_specs=pl.BlockSpec((None, rows, 2 * V_HEAD_DIM), lambda b, hp, g: (b, g, hp)),
        out_shape=jax.ShapeDtypeStruct((B, S, D_MLA), BF16),
        scratch_shapes=[pltpu.VMEM((2, T, T), F32), pltpu.VMEM((2, T, T), F32)],
        compiler_params=pltpu.CompilerParams(
            dimension_semantics=("arbitrary", "arbitrary", "arbitrary")),
    )(q, k, vt, gbs)


def _out_kernel(x_ref, ya_ref, yb_ref, yc_ref, w_ref, fg_ref, o_ref):
    y = jnp.concatenate([ya_ref[...], yb_ref[...], yc_ref[...]], axis=1)
    o_ref[...] = _rms(x_ref[...] + _dot(y, w_ref[...]), fg_ref[...])


def _out_call(x, ya, yb, yc, w_out, layer, final_g):
    B, S, _ = x.shape
    ts = T_OUT
    row_blk = lambda w: pl.BlockSpec((None, ts, w), lambda b, t: (b, t, 0))
    return pl.pallas_call(
        _out_kernel,
        grid=(B, S // ts),
        in_specs=[row_blk(D_MODEL), row_blk(D_LRU), row_blk(D_MLA), row_blk(D_POOL),
                  _layer_spec(w_out, layer), pl.BlockSpec(final_g.shape, lambda b, t: (0, 0))],
        out_specs=row_blk(D_MODEL),
        out_shape=jax.ShapeDtypeStruct((B, S, D_MODEL), F32),
        compiler_params=pltpu.CompilerParams(dimension_semantics=("arbitrary", "arbitrary")),
    )(x, ya, yb, yc, w_out, final_g)


def _block_diag(w):
    dp, g, n, _ = w.shape
    eye = jnp.eye(g, dtype=w.dtype)
    return jnp.einsum("dgij,gh->dgihj", w, eye).reshape(dp, g * n, g * n)


def _pack_weights(norm_g, w_in, conv_w, conv_b, w_rg, b_rg, w_ig, b_ig, lru_lambda,
                  q_norm_g, w_uq, kv_norm_g, w_ukv, w_pool, pool_scale, w_out):
    kr_lo = int(np.sum(IN_SIZES[:4]))
    kr_hi = kr_lo + QK_ROPE_DIM
    zeros = lambda n: jnp.zeros(w_in.shape[:2] + (n,), BF16)
    w_in_b = w_in.astype(BF16)
    w_in_p = jnp.concatenate([w_in_b[:, :, :kr_lo], zeros(ROPE_LO), w_in_b[:, :, kr_lo:kr_hi],
                              zeros(HEAD_PAD - ROPE_LO - QK_ROPE_DIM), w_in_b[:, :, kr_hi:]], axis=-1)
    assert w_in_p.shape[-1] == D_IN_PAD
    w_gate = jnp.concatenate([_block_diag(w_rg), _block_diag(w_ig)], axis=-1).astype(BF16)
    b_gate = jnp.concatenate([b_rg, b_ig], axis=-1)[:, None, :]
    uq = w_uq.reshape(DEPTH, Q_LORA_RANK, MLA_HEADS, QK_NOPE_DIM + QK_ROPE_DIM)
    uq = jnp.pad(uq, ((0, 0), (0, 0), (0, 0), (0, HEAD_PAD - QK_NOPE_DIM - QK_ROPE_DIM)))
    ukv = w_ukv.reshape(DEPTH, KV_LORA_RANK, MLA_HEADS, QK_NOPE_DIM + V_HEAD_DIM)
    uk = jnp.pad(ukv[..., :QK_NOPE_DIM], ((0, 0), (0, 0), (0, 0), (0, HEAD_PAD - QK_NOPE_DIM)))
    uv = ukv[..., QK_NOPE_DIM:].reshape(DEPTH, KV_LORA_RANK, D_MLA)
    return {
        "norm_g": norm_g[:, None, :],
        "w_in": w_in_p,
        "conv_w": conv_w,
        "conv_b": conv_b[:, None, :],
        "w_gate": w_gate,
        "b_gate": b_gate,
        "lam": lru_lambda[:, None, :],
        "q_norm_g": q_norm_g[:, None, :],
        "w_uq": uq.reshape(DEPTH, Q_LORA_RANK, D_QK).astype(BF16),
        "kv_norm_g": kv_norm_g[:, None, :],
        "w_uk": uk.reshape(DEPTH, KV_LORA_RANK, D_QK).astype(BF16),
        "w_vt": jnp.swapaxes(uv, 1, 2).astype(BF16),
        "w_pool": _block_diag(w_pool).astype(BF16),
        "pool_scale": pool_scale[:, None, :],
        "w_out": w_out.astype(BF16),
    }


def _rope_tables(seq_len):
    pos = jnp.arange(seq_len, dtype=F32)
    inv_freq = ROPE_BASE ** (-jnp.arange(0, QK_ROPE_DIM, 2, dtype=F32) / QK_ROPE_DIM)
    ang = pos[:, None] * inv_freq[None, :]
    cos, sin = jnp.cos(ang), jnp.sin(ang)
    zeros = jnp.zeros_like(sin)
    tail = jnp.zeros((seq_len, HEAD_PAD - ROPE_LO - QK_ROPE_DIM), F32)
    ones = jnp.ones((seq_len, ROPE_LO), F32)
    lead = jnp.zeros((seq_len, ROPE_LO), F32)
    cos_t = jnp.concatenate([ones, cos, cos, tail], axis=1)
    sin_hi = jnp.concatenate([lead, -sin, zeros, tail], axis=1)
    sin_lo = jnp.concatenate([lead, zeros, sin, tail], axis=1)
    return cos_t, sin_hi, sin_lo


def kernel(x, norm_g, w_in, conv_w, conv_b, w_rg, b_rg, w_ig, b_ig, lru_lambda, q_norm_g, w_uq,
           kv_norm_g, w_ukv, w_pool, pool_scale, w_out, final_norm_g):
    B, S, D = x.shape
    assert D == D_MODEL and S % T_PROJ == 0 and S % T_ATT == 0 and T_PROJ % T_ATT == 0
    packed = _pack_weights(norm_g, w_in, conv_w, conv_b, w_rg, b_rg, w_ig, b_ig, lru_lambda,
                           q_norm_g, w_uq, kv_norm_g, w_ukv, w_pool, pool_scale, w_out)
    tabs = _rope_tables(S)
    final_g = final_norm_g[None, :]
    mixed = None
    for l in range(DEPTH):
        if mixed is None:
            ya, gbs, yc, q, k, vt = _proj_call(x, packed, l, tabs, T_ATT)
        else:
            x, ya, gbs, yc, q, k, vt = _proj_call(x, packed, l, tabs, T_ATT, prev=mixed)
        mixed = (ya, _attn_call(q, k, vt, gbs), yc)
    return _out_call(x, *mixed, packed["w_out"], DEPTH - 1, final_g)
```

```python
import functools
import itertools

import jax
import jax.numpy as jnp
import numpy as np
from jax import lax
from jax.experimental import pallas as pl
from jax.experimental.pallas import tpu as pltpu

F32 = jnp.float32
BF16 = jnp.bfloat16

D_MODEL = 1024
DEPTH = 4
EPS = 1e-6
D_LRU = 384
LRU_HEADS = 6
LRU_HEAD_DIM = 64
CONV_WIDTH = 4
LRU_C = 8.0
MLA_HEADS = 6
QK_NOPE_DIM = 64
QK_ROPE_DIM = 32
V_HEAD_DIM = 64
D_MLA = MLA_HEADS * V_HEAD_DIM
Q_LORA_RANK = 384
KV_LORA_RANK = 256
ROPE_BASE = 10000.0
POOL_WINDOWS = (2, 4, 8, 16)
POOL_GROUP_DIM = 64
D_POOL = POOL_GROUP_DIM * len(POOL_WINDOWS)
D_MIX = D_LRU + D_MLA + D_POOL
IN_SIZES = (D_LRU, D_LRU, Q_LORA_RANK, KV_LORA_RANK, QK_ROPE_DIM, D_MLA, D_POOL, D_POOL)

LANES = 128
SUBLANES = 8
HEAD_PAD = LANES
D_QK = MLA_HEADS * HEAD_PAD
ROPE_LO = QK_NOPE_DIM
HALF = QK_ROPE_DIM // 2
CONV_HALO = 8
POOL_HALO = 16
NEG = -1e30
BF16_SUBLANES = 16
VT_ROWS = V_HEAD_DIM + BF16_SUBLANES
Q_SCALE = (QK_NOPE_DIM + QK_ROPE_DIM) ** -0.5 * float(np.log2(np.e))

C_ZA = 0
C_GA = C_ZA + D_LRU
C_CQ = C_GA + D_LRU
C_CKV = C_CQ + Q_LORA_RANK
C_KR = C_CKV + KV_LORA_RANK
C_GB = C_KR + HEAD_PAD
C_ZC = C_GB + D_MLA
C_GC = C_ZC + D_POOL
D_IN_PAD = C_GC + D_POOL

T_PROJ = 512
T_ATT = 512
T_OUT = 1024
Q_CHUNK = 256
Q_TILES_PER_STEP = 4
assert Q_TILES_PER_STEP == 4
SCORE_LEAD = 1


def _dot(a, b):
    return jnp.dot(a, b, preferred_element_type=F32)


def _dot_nt(a, b):
    return lax.dot_general(a, b, (((1,), (1,)), ((), ())), preferred_element_type=F32)


def _rms(x, g):
    return x * lax.rsqrt(jnp.mean(x * x, axis=-1, keepdims=True) + EPS) * g


def _silu(x):
    return x * jax.nn.sigmoid(x)


def _rope(x, cos, sin_hi, sin_lo):
    return (x * cos + pltpu.roll(x, LANES - HALF, axis=1) * sin_hi
            + pltpu.roll(x, HALF, axis=1) * sin_lo)


N_PROJ_PARAMS = 17


def _proj_kernel(x_ref, *refs):
    _proj_body(x_ref[...], *refs)


def _proj_fused_kernel(x_ref, ya_prev_ref, yb_prev_ref, yc_prev_ref, wout_ref, *refs):
    y = jnp.concatenate([ya_prev_ref[...], yb_prev_ref[...], yc_prev_ref[...]], axis=1)
    x = x_ref[...] + _dot(y, wout_ref[...])
    xo_ref = refs[N_PROJ_PARAMS]
    xo_ref[...] = x
    _proj_body(x, *refs[:N_PROJ_PARAMS], *refs[N_PROJ_PARAMS + 1:])


def _proj_body(x, ng_ref, win_ref, convw_ref, convb_ref, wg_ref, bg_ref, lam_ref,
               qg_ref, wuq_ref, kvg_ref, wuk_ref, wvt_ref, wpool_ref, pscale_ref,
               cos_ref, shi_ref, slo_ref,
               ya_ref, gbs_ref, yc_ref, q_ref, k_ref, vt_ref,
               za_ext, zc_ext, h_carry):
    ts = x.shape[0]
    t = pl.program_id(1)

    @pl.when(t == 0)
    def _():
        za_ext[0:CONV_HALO, :] = jnp.zeros((CONV_HALO, D_LRU), F32)
        zc_ext[0:POOL_HALO, :] = jnp.zeros((POOL_HALO, D_POOL), F32)
        h_carry[...] = jnp.zeros_like(h_carry)

    hb = _rms(x, ng_ref[...]).astype(BF16)

    zag = _dot(hb, win_ref[:, C_ZA:C_CQ])
    za = zag[:, :D_LRU]
    ga = zag[:, D_LRU:]
    za_ext[CONV_HALO:CONV_HALO + ts, :] = za
    xa = convb_ref[...]
    for k in range(CONV_WIDTH):
        off = CONV_HALO - (CONV_WIDTH - 1) + k
        xa = xa + za_ext[off:off + ts, :] * convw_ref[k:k + 1, :]
    za_ext[0:CONV_HALO, :] = za_ext[ts:ts + CONV_HALO, :]
    gate_pre = _dot(xa.astype(BF16), wg_ref[...]) + bg_ref[...]

    lat = _dot(hb, win_ref[:, C_CQ:C_GB])
    cq = lat[:, :C_CKV - C_CQ]
    ckv = lat[:, C_CKV - C_CQ:C_KR - C_CQ]
    kr_pre = lat[:, C_KR - C_CQ:]
    gates = jax.nn.sigmoid(gate_pre)
    r = gates[:, :D_LRU]
    i = gates[:, D_LRU:]
    nl = -lam_ref[...]
    softplus = jnp.maximum(nl, 0.0) + jnp.log1p(jnp.exp(-jnp.abs(nl)))
    log_a = (-LRU_C) * r * softplus
    a = jnp.exp(log_a)
    h = jnp.sqrt(-jnp.tanh(log_a) * (a * a + 1.0)) * (i * xa)

    qf = _dot(_rms(cq, qg_ref[...]).astype(BF16), wuq_ref[...])
    kvn = _rms(ckv, kvg_ref[...]).astype(BF16)

    row = lax.broadcasted_iota(jnp.int32, (ts, 1), 0)
    matmuls = iter((
        lambda: _dot(kvn, wuk_ref[...]),
        lambda: _dot_nt(wvt_ref[...], kvn),
        lambda: _dot(hb, win_ref[:, C_GB:D_IN_PAD]),
    ))
    products = []
    d = 1
    while d < ts:
        if d < SUBLANES:
            keep = row >= d
            a_sh = jnp.where(keep, pltpu.roll(a, d, axis=0), 1.0)
            h_sh = jnp.where(keep, pltpu.roll(h, d, axis=0), 0.0)
            h = a * h_sh + h
            a = a * a_sh
        else:
            h = jnp.concatenate([h[:d], a[d:] * h[:ts - d] + h[d:]], axis=0)
            a = jnp.concatenate([a[:d], a[d:] * a[:ts - d]], axis=0)
        if d >= SUBLANES // 2:
            products.extend(f() for f in itertools.islice(matmuls, 1))
        d *= 2
    kn, vt, gbz = products + [f() for f in matmuls]
    gb = gbz[:, :C_ZC - C_GB]
    zc = gbz[:, C_ZC - C_GB:C_GC - C_GB]
    gc = gbz[:, C_GC - C_GB:]
    h = h + a * h_carry[...]
    h_carry[...] = h[ts - 1:ts, :]
    ya_ref[...] = (h * _silu(ga)).astype(BF16)

    cos = cos_ref[...]
    shi = shi_ref[...]
    slo = slo_ref[...]
    kr = _rope(kr_pre, cos, shi, slo)
    for hd in range(MLA_HEADS):
        sl = slice(hd * HEAD_PAD, (hd + 1) * HEAD_PAD)
        q_ref[:, sl] = (_rope(qf[:, sl], cos, shi, slo) * Q_SCALE).astype(BF16)
        k_ref[:, sl] = (kn[:, sl] + kr).astype(BF16)
    tk = vt_ref.shape[-1]
    ones_rows = (lax.broadcasted_iota(jnp.int32, (VT_ROWS - V_HEAD_DIM, tk), 0) == 0).astype(BF16)
    for hd in range(MLA_HEADS):
        for c in range(ts // tk):
            vt_ref[hd, c, 0:V_HEAD_DIM, :] = (
                vt[hd * V_HEAD_DIM:(hd + 1) * V_HEAD_DIM, c * tk:(c + 1) * tk].astype(BF16))
            vt_ref[hd, c, V_HEAD_DIM:VT_ROWS, :] = ones_rows
    gbs_ref[...] = _silu(gb)

    zc_ext[POOL_HALO:POOL_HALO + ts, :] = zc
    lane = lax.broadcasted_iota(jnp.int32, (1, LANES), 1)
    pos1 = (t * ts + row + 1).astype(F32)
    pooled = []
    for blk in range(D_POOL // LANES):
        w_small, w_big = POOL_WINDOWS[2 * blk], POOL_WINDOWS[2 * blk + 1]
        cols = slice(blk * LANES, (blk + 1) * LANES)
        acc = zc_ext[:, cols]
        sums = {}
        w = 1
        while w < w_big:
            acc = acc + pltpu.roll(acc, w, axis=0)
            w *= 2
            sums[w] = acc[POOL_HALO:, :]
        first = lane < POOL_GROUP_DIM
        wsum = jnp.where(first, sums[w_small], sums[w_big])
        cnt = jnp.minimum(pos1, jnp.where(first, float(w_small), float(w_big)))
        pooled.append(wsum / cnt - zc[:, cols])
    zc_ext[0:POOL_HALO, :] = zc_ext[ts:ts + POOL_HALO, :]
    pooled = jnp.concatenate(pooled, axis=1).astype(BF16)
    yc_ref[...] = (_dot(pooled, wpool_ref[...]) * pscale_ref[...] * _silu(gc)).astype(BF16)


def _layer_spec(a, layer):
    return pl.BlockSpec((None,) + a.shape[1:], lambda b, t: (layer,) + (0,) * (a.ndim - 1))


def _proj_call(x, packed, layer, tabs, t_att, prev=None):
    B, S, _ = x.shape
    ts = T_PROJ
    nt = S // ts
    row_blk = lambda w: pl.BlockSpec((None, ts, w), lambda b, t: (b, t, 0))
    full = lambda a: _layer_spec(a, layer)
    tab = pl.BlockSpec((ts, LANES), lambda b, t: (t, 0))
    weights = [packed[n] for n in ("norm_g", "w_in", "conv_w", "conv_b", "w_gate", "b_gate", "lam",
                                   "q_norm_g", "w_uq", "kv_norm_g", "w_uk", "w_vt", "w_pool", "pool_scale")]
    out_shape = (
        jax.ShapeDtypeStruct((B, S, D_LRU), BF16),
        jax.ShapeDtypeStruct((B, S, D_MLA), F32),
        jax.ShapeDtypeStruct((B, S, D_POOL), BF16),
        jax.ShapeDtypeStruct((B, S, D_QK), BF16),
        jax.ShapeDtypeStruct((B, S, D_QK), BF16),
        jax.ShapeDtypeStruct((B, MLA_HEADS, S // t_att, VT_ROWS, t_att), BF16),
    )
    out_specs = (
        row_blk(D_LRU), row_blk(D_MLA), row_blk(D_POOL), row_blk(D_QK), row_blk(D_QK),
        pl.BlockSpec((None, MLA_HEADS, ts // t_att, VT_ROWS, t_att), lambda b, t: (b, 0, t, 0, 0)),
    )
    assert len(weights) + len(tabs) == N_PROJ_PARAMS
    param_specs = [full(w) for w in weights] + [tab, tab, tab]
    if prev is None:
        body, lead, lead_specs = _proj_kernel, (x,), [row_blk(D_MODEL)]
    else:
        body = _proj_fused_kernel
        lead = (x, *prev, packed["w_out"])
        lead_specs = [row_blk(D_MODEL), row_blk(D_LRU), row_blk(D_MLA), row_blk(D_POOL),
                      _layer_spec(packed["w_out"], layer - 1)]
        out_shape = (jax.ShapeDtypeStruct((B, S, D_MODEL), F32),) + out_shape
        out_specs = (row_blk(D_MODEL),) + out_specs
    return pl.pallas_call(
        body,
        grid=(B, nt),
        in_specs=lead_specs + param_specs,
        out_specs=out_specs,
        out_shape=out_shape,
        scratch_shapes=[
            pltpu.VMEM((ts + CONV_HALO, D_LRU), F32),
            pltpu.VMEM((ts + POOL_HALO, D_POOL), F32),
            pltpu.VMEM((1, D_LRU), F32),
        ],
        compiler_params=pltpu.CompilerParams(dimension_semantics=("arbitrary", "arbitrary")),
    )(*lead, *weights, *tabs)


def _attn_kernel(q_ref, k_ref, vt_ref, g_ref, o_ref, s_a, s_b):
    T = vt_ref.shape[-1]
    g = pl.program_id(2)
    n_c = T // Q_CHUNK
    chains = [(h, c) for h in range(2) for c in range(n_c)]
    n_chains = len(chains)

    def score_chain(r, j, n, nxt):
        h, c = chains[n]
        q = q_ref[r * T + c * Q_CHUNK:r * T + (c + 1) * Q_CHUNK, h * HEAD_PAD:(h + 1) * HEAD_PAD]
        s = _dot_nt(k_ref[pl.ds(pl.multiple_of(j * T, T), T), h * HEAD_PAD:(h + 1) * HEAD_PAD], q)
        nxt[h, :, c * Q_CHUNK:(c + 1) * Q_CHUNK] = s
        return jnp.max(s, axis=0, keepdims=True)

    def update_chain(j, n, cur, state, cmax, diag):
        h, c = chains[n]
        m, acc = state[n]
        rows = (c + 1) * Q_CHUNK if diag else T

        def load():
            s = cur[h, 0:rows, c * Q_CHUNK:(c + 1) * Q_CHUNK]
            if diag:
                kv_pos = lax.broadcasted_iota(jnp.int32, (rows, Q_CHUNK), 0)
                q_pos = lax.broadcasted_iota(jnp.int32, (rows, Q_CHUNK), 1) + c * Q_CHUNK
                s = jnp.where(kv_pos <= q_pos, s, NEG)
            return s

        tile_max = jnp.max(load(), axis=0, keepdims=True) if diag else cmax[n]
        m_new = jnp.maximum(m, tile_max)
        alpha = jnp.exp2(m - m_new)
        p = jnp.exp2(load() - m_new).astype(BF16)
        return m_new, alpha * acc + _dot(vt_ref[h, j, :, 0:rows], p)

    def stage(r, j, cur, nxt, carry):
        state, cmax = carry
        new_state, new_cmax = [], []
        for n in range(SCORE_LEAD):
            new_cmax.append(score_chain(r, j + 1, n, nxt))
        for n in range(n_chains):
            if n + SCORE_LEAD < n_chains:
                new_cmax.append(score_chain(r, j + 1, n + SCORE_LEAD, nxt))
            new_state.append(update_chain(j, n, cur, state, cmax, False))
        return tuple(new_state), tuple(new_cmax)

    def finish(r, j, cur, carry, nxt):
        state, new_cmax = [], []
        for n in range(n_chains):
            if nxt is not None:
                new_cmax.append(score_chain(r + 1, 0, n, nxt))
            state.append(update_chain(j, n, cur, carry[0], None, True))
        outs = [acc[:V_HEAD_DIM] * (1.0 / acc[V_HEAD_DIM:V_HEAD_DIM + 1]) for _, acc in state]
        out_t = jnp.concatenate(
            [jnp.concatenate(outs[h * n_c:(h + 1) * n_c], axis=1) for h in range(2)], axis=0)
        o_ref[r * T:(r + 1) * T, :] = (out_t.T * g_ref[r * T:(r + 1) * T, :]).astype(BF16)
        return tuple(new_cmax)

    init = tuple((jnp.full((1, Q_CHUNK), NEG, F32), jnp.zeros((VT_ROWS, Q_CHUNK), F32))
                 for _ in chains)
    cmax = tuple(score_chain(0, 0, n, s_a) for n in range(n_chains))
    first, other = s_a, s_b
    for r in range(Q_TILES_PER_STEP):
        n_full = Q_TILES_PER_STEP * g + r
        x, y = first, other

        def pair(j, carry, r=r, x=x, y=y):
            return stage(r, j + 1, y, x, stage(r, j, x, y, carry))

        def quad(u, carry):
            return pair(4 * u + 2, pair(4 * u, carry))

        carry = lax.fori_loop(0, g, quad, (init, cmax))
        if r >= 2:
            carry = pair(n_full - r, carry)
        if r % 2:
            carry = stage(r, n_full - 1, x, y, carry)
            x, y = y, x
        last = r == Q_TILES_PER_STEP - 1
        cmax = finish(r, n_full, x, carry, None if last else y)
        first, other = y, x


def _attn_call(q, k, vt, gbs):
    B, S, _ = q.shape
    T = vt.shape[-1]
    nq = S // T
    rows = Q_TILES_PER_STEP * T
    pair = 2 * HEAD_PAD
    return pl.pallas_call(
        _attn_kernel,
        grid=(B, MLA_HEADS // 2, nq // Q_TILES_PER_STEP),
        in_specs=[
            pl.BlockSpec((None, rows, pair), lambda b, hp, g: (b, g, hp)),
            pl.BlockSpec((None, S, pair), lambda b, hp, g: (b, 0, hp)),
            pl.BlockSpec((None, 2, nq, VT_ROWS, T), lambda b, hp, g: (b, hp, 0, 0, 0)),
            pl.BlockSpec((None, rows, 2 * V_HEAD_DIM), lambda b, hp, g: (b, g, hp)),
        ],
        out_specs=pl.BlockSpec((None, rows, 2 * V_HEAD_DIM), lambda b, hp, g: (b, g, hp)),
        out_shape=jax.ShapeDtypeStruct((B, S, D_MLA), BF16),
        scratch_shapes=[pltpu.VMEM((2, T, T), F32), pltpu.VMEM((2, T, T), F32)],
        compiler_params=pltpu.CompilerParams(
            dimension_semantics=("arbitrary", "arbitrary", "arbitrary")),
    )(q, k, vt, gbs)


def _out_kernel(x_ref, ya_ref, yb_ref, yc_ref, w_ref, fg_ref, o_ref):
    y = jnp.concatenate([ya_ref[...], yb_ref[...], yc_ref[...]], axis=1)
    o_ref[...] = _rms(x_ref[...] + _dot(y, w_ref[...]), fg_ref[...])


def _out_call(x, ya, yb, yc, w_out, layer, final_g):
    B, S, _ = x.shape
    ts = T_OUT
    row_blk = lambda w: pl.BlockSpec((None, ts, w), lambda b, t: (b, t, 0))
    return pl.pallas_call(
        _out_kernel,
        grid=(B, S // ts),
        in_specs=[row_blk(D_MODEL), row_blk(D_LRU), row_blk(D_MLA), row_blk(D_POOL),
                  _layer_spec(w_out, layer), pl.BlockSpec(final_g.shape, lambda b, t: (0, 0))],
        out_specs=row_blk(D_MODEL),
        out_shape=jax.ShapeDtypeStruct((B, S, D_MODEL), F32),
        compiler_params=pltpu.CompilerParams(dimension_semantics=("arbitrary", "arbitrary")),
    )(x, ya, yb, yc, w_out, final_g)


def _block_diag(w):
    dp, g, n, _ = w.shape
    eye = jnp.eye(g, dtype=w.dtype)
    return jnp.einsum("dgij,gh->dgihj", w, eye).reshape(dp, g * n, g * n)


def _pack_weights(norm_g, w_in, conv_w, conv_b, w_rg, b_rg, w_ig, b_ig, lru_lambda,
                  q_norm_g, w_uq, kv_norm_g, w_ukv, w_pool, pool_scale, w_out):
    kr_lo = int(np.sum(IN_SIZES[:4]))
    kr_hi = kr_lo + QK_ROPE_DIM
    assert kr_lo == C_KR and D_IN_PAD - C_GB == w_in.shape[-1] - kr_hi
    w_in_p = jnp.zeros(w_in.shape[:2] + (D_IN_PAD,), BF16)
    w_in_p = w_in_p.at[:, :, :C_KR].set(w_in[:, :, :kr_lo].astype(BF16))
    w_in_p = w_in_p.at[:, :, C_KR + ROPE_LO:C_KR + ROPE_LO + QK_ROPE_DIM].set(w_in[:, :, kr_lo:kr_hi].astype(BF16))
    w_in_p = w_in_p.at[:, :, C_GB:].set(w_in[:, :, kr_hi:].astype(BF16))
    w_gate = jnp.concatenate([_block_diag(w_rg), _block_diag(w_ig)], axis=-1).astype(BF16)
    b_gate = jnp.concatenate([b_rg, b_ig], axis=-1)[:, None, :]
    uq = w_uq.reshape(DEPTH, Q_LORA_RANK, MLA_HEADS, QK_NOPE_DIM + QK_ROPE_DIM)
    uq = jnp.pad(uq, ((0, 0), (0, 0), (0, 0), (0, HEAD_PAD - QK_NOPE_DIM - QK_ROPE_DIM)))
    ukv = w_ukv.reshape(DEPTH, KV_LORA_RANK, MLA_HEADS, QK_NOPE_DIM + V_HEAD_DIM)
    uk = jnp.pad(ukv[..., :QK_NOPE_DIM], ((0, 0), (0, 0), (0, 0), (0, HEAD_PAD - QK_NOPE_DIM)))
    uv = ukv[..., QK_NOPE_DIM:].reshape(DEPTH, KV_LORA_RANK, D_MLA)
    return {
        "norm_g": norm_g[:, None, :],
        "w_in": w_in_p,
        "conv_w": conv_w,
        "conv_b": conv_b[:, None, :],
        "w_gate": w_gate,
        "b_gate": b_gate,
        "lam": lru_lambda[:, None, :],
        "q_norm_g": q_norm_g[:, None, :],
        "w_uq": uq.reshape(DEPTH, Q_LORA_RANK, D_QK).astype(BF16),
        "kv_norm_g": kv_norm_g[:, None, :],
        "w_uk": uk.reshape(DEPTH, KV_LORA_RANK, D_QK).astype(BF16),
        "w_vt": jnp.swapaxes(uv, 1, 2).astype(BF16),
        "w_pool": _block_diag(w_pool).astype(BF16),
        "pool_scale": pool_scale[:, None, :],
        "w_out": w_out.astype(BF16),
    }


def _rope_tables(seq_len):
    pos = jnp.arange(seq_len, dtype=F32)
    inv_freq = ROPE_BASE ** (-jnp.arange(0, QK_ROPE_DIM, 2, dtype=F32) / QK_ROPE_DIM)
    ang = pos[:, None] * inv_freq[None, :]
    cos, sin = jnp.cos(ang), jnp.sin(ang)
    zeros = jnp.zeros_like(sin)
    tail = jnp.zeros((seq_len, HEAD_PAD - ROPE_LO - QK_ROPE_DIM), F32)
    ones = jnp.ones((seq_len, ROPE_LO), F32)
    lead = jnp.zeros((seq_len, ROPE_LO), F32)
    cos_t = jnp.concatenate([ones, cos, cos, tail], axis=1)
    sin_hi = jnp.concatenate([lead, -sin, zeros, tail], axis=1)
    sin_lo = jnp.concatenate([lead, zeros, sin, tail], axis=1)
    return cos_t, sin_hi, sin_lo


def kernel(x, norm_g, w_in, conv_w, conv_b, w_rg, b_rg, w_ig, b_ig, lru_lambda, q_norm_g, w_uq,
           kv_norm_g, w_ukv, w_pool, pool_scale, w_out, final_norm_g):
    B, S, D = x.shape
    assert D == D_MODEL and S % T_PROJ == 0 and S % T_ATT == 0 and T_PROJ % T_ATT == 0
    packed = _pack_weights(norm_g, w_in, conv_w, conv_b, w_rg, b_rg, w_ig, b_ig, lru_lambda,
                           q_norm_g, w_uq, kv_norm_g, w_ukv, w_pool, pool_scale, w_out)
    tabs = _rope_tables(S)
    final_g = final_norm_g[None, :]
    mixed = None
    for l in range(DEPTH):
        if mixed is None:
            ya, gbs, yc, q, k, vt = _proj_call(x, packed, l, tabs, T_ATT)
        else:
            x, ya, gbs, yc, q, k, vt = _proj_call(x, packed, l, tabs, T_ATT, prev=mixed)
        mixed = (ya, _attn_call(q, k, vt, gbs), yc)
    return _out_call(x, *mixed, packed["w_out"], DEPTH - 1, final_g)
```

```python
import functools
import itertools

import jax
import jax.numpy as jnp
import numpy as np
from jax import lax
from jax.experimental import pallas as pl
from jax.experimental.pallas import tpu as pltpu

F32 = jnp.float32
BF16 = jnp.bfloat16

D_MODEL = 1024
DEPTH = 4
EPS = 1e-6
D_LRU = 384
LRU_HEADS = 6
LRU_HEAD_DIM = 64
CONV_WIDTH = 4
LRU_C = 8.0
MLA_HEADS = 6
QK_NOPE_DIM = 64
QK_ROPE_DIM = 32
V_HEAD_DIM = 64
D_MLA = MLA_HEADS * V_HEAD_DIM
Q_LORA_RANK = 384
KV_LORA_RANK = 256
ROPE_BASE = 10000.0
POOL_WINDOWS = (2, 4, 8, 16)
POOL_GROUP_DIM = 64
D_POOL = POOL_GROUP_DIM * len(POOL_WINDOWS)
D_MIX = D_LRU + D_MLA + D_POOL
IN_SIZES = (D_LRU, D_LRU, Q_LORA_RANK, KV_LORA_RANK, QK_ROPE_DIM, D_MLA, D_POOL, D_POOL)

LANES = 128
SUBLANES = 8
HEAD_PAD = LANES
D_QK = MLA_HEADS * HEAD_PAD
ROPE_LO = QK_NOPE_DIM
HALF = QK_ROPE_DIM // 2
CONV_HALO = 8
POOL_HALO = 16
NEG = -1e30
BF16_SUBLANES = 16
VT_ROWS = V_HEAD_DIM + BF16_SUBLANES
Q_SCALE = (QK_NOPE_DIM + QK_ROPE_DIM) ** -0.5 * float(np.log2(np.e))

C_ZA = 0
C_GA = C_ZA + D_LRU
C_CQ = C_GA + D_LRU
C_CKV = C_CQ + Q_LORA_RANK
C_KR = C_CKV + KV_LORA_RANK
C_GB = C_KR + HEAD_PAD
C_ZC = C_GB + D_MLA
C_GC = C_ZC + D_POOL
D_IN_PAD = C_GC + D_POOL

T_PROJ = 1024
T_ATT = 512
T_OUT = 1024
Q_CHUNK = 256
Q_TILES_PER_STEP = 4
assert Q_TILES_PER_STEP == 4


def _dot(a, b):
    return jnp.dot(a, b, preferred_element_type=F32)


def _dot_nt(a, b):
    return lax.dot_general(a, b, (((1,), (1,)), ((), ())), preferred_element_type=F32)


def _rms(x, g):
    return x * lax.rsqrt(jnp.mean(x * x, axis=-1, keepdims=True) + EPS) * g


def _silu(x):
    return x * jax.nn.sigmoid(x)


def _rope(x, cos, sin_hi, sin_lo):
    return (x * cos + pltpu.roll(x, LANES - HALF, axis=1) * sin_hi
            + pltpu.roll(x, HALF, axis=1) * sin_lo)


N_PROJ_PARAMS = 17


def _proj_kernel(x_ref, *refs):
    _proj_body(x_ref[...], *refs)


def _proj_fused_kernel(x_ref, ya_prev_ref, yb_prev_ref, yc_prev_ref, wout_ref, *refs):
    y = jnp.concatenate([ya_prev_ref[...], yb_prev_ref[...], yc_prev_ref[...]], axis=1)
    x = x_ref[...] + _dot(y, wout_ref[...])
    xo_ref = refs[N_PROJ_PARAMS]
    xo_ref[...] = x
    _proj_body(x, *refs[:N_PROJ_PARAMS], *refs[N_PROJ_PARAMS + 1:])


def _proj_body(x, ng_ref, win_ref, convw_ref, convb_ref, wg_ref, bg_ref, lam_ref,
               qg_ref, wuq_ref, kvg_ref, wuk_ref, wvt_ref, wpool_ref, pscale_ref,
               cos_ref, shi_ref, slo_ref,
               ya_ref, gbs_ref, yc_ref, q_ref, k_ref, vt_ref,
               za_ext, zc_ext, h_carry):
    ts = x.shape[0]
    t = pl.program_id(1)

    @pl.when(t == 0)
    def _():
        za_ext[0:CONV_HALO, :] = jnp.zeros((CONV_HALO, D_LRU), F32)
        zc_ext[0:POOL_HALO, :] = jnp.zeros((POOL_HALO, D_POOL), F32)
        h_carry[...] = jnp.zeros_like(h_carry)

    hb = _rms(x, ng_ref[...]).astype(BF16)

    zag = _dot(hb, win_ref[:, C_ZA:C_CQ])
    za = zag[:, :D_LRU]
    ga = zag[:, D_LRU:]
    za_ext[CONV_HALO:CONV_HALO + ts, :] = za
    xa = convb_ref[...]
    for k in range(CONV_WIDTH):
        off = CONV_HALO - (CONV_WIDTH - 1) + k
        xa = xa + za_ext[off:off + ts, :] * convw_ref[k:k + 1, :]
    za_ext[0:CONV_HALO, :] = za_ext[ts:ts + CONV_HALO, :]
    gate_pre = _dot(xa.astype(BF16), wg_ref[...]) + bg_ref[...]

    lat = _dot(hb, win_ref[:, C_CQ:C_GB])
    cq = lat[:, :C_CKV - C_CQ]
    ckv = lat[:, C_CKV - C_CQ:C_KR - C_CQ]
    kr_pre = lat[:, C_KR - C_CQ:]
    gates = jax.nn.sigmoid(gate_pre)
    r = gates[:, :D_LRU]
    i = gates[:, D_LRU:]
    nl = -lam_ref[...]
    softplus = jnp.maximum(nl, 0.0) + jnp.log1p(jnp.exp(-jnp.abs(nl)))
    log_a = (-LRU_C) * r * softplus
    a = jnp.exp(log_a)
    h = jnp.sqrt(-jnp.tanh(log_a) * (a * a + 1.0)) * (i * xa)

    qf = _dot(_rms(cq, qg_ref[...]).astype(BF16), wuq_ref[...])
    kvn = _rms(ckv, kvg_ref[...]).astype(BF16)

    row = lax.broadcasted_iota(jnp.int32, (ts, 1), 0)
    matmuls = iter((
        lambda: _dot(kvn, wuk_ref[...]),
        lambda: _dot_nt(wvt_ref[...], kvn),
        lambda: _dot(hb, win_ref[:, C_GB:D_IN_PAD]),
    ))
    products = []
    d = 1
    while d < ts:
        if d < SUBLANES:
            keep = row >= d
            a_sh = jnp.where(keep, pltpu.roll(a, d, axis=0), 1.0)
            h_sh = jnp.where(keep, pltpu.roll(h, d, axis=0), 0.0)
            h = a * h_sh + h
            a = a * a_sh
        else:
            h = jnp.concatenate([h[:d], a[d:] * h[:ts - d] + h[d:]], axis=0)
            a = jnp.concatenate([a[:d], a[d:] * a[:ts - d]], axis=0)
        if d >= SUBLANES // 2:
            products.extend(f() for f in itertools.islice(matmuls, 1))
        d *= 2
    kn, vt, gbz = products + [f() for f in matmuls]
    gb = gbz[:, :C_ZC - C_GB]
    zc = gbz[:, C_ZC - C_GB:C_GC - C_GB]
    gc = gbz[:, C_GC - C_GB:]
    h = h + a * h_carry[...]
    h_carry[...] = h[ts - 1:ts, :]
    ya_ref[...] = (h * _silu(ga)).astype(BF16)

    cos = cos_ref[...]
    shi = shi_ref[...]
    slo = slo_ref[...]
    kr = _rope(kr_pre, cos, shi, slo)
    for hd in range(MLA_HEADS):
        sl = slice(hd * HEAD_PAD, (hd + 1) * HEAD_PAD)
        q_ref[:, sl] = (_rope(qf[:, sl], cos, shi, slo) * Q_SCALE).astype(BF16)
        k_ref[:, sl] = (kn[:, sl] + kr).astype(BF16)
    tk = vt_ref.shape[-1]
    ones_rows = (lax.broadcasted_iota(jnp.int32, (VT_ROWS - V_HEAD_DIM, tk), 0) == 0).astype(BF16)
    for hd in range(MLA_HEADS):
        for c in range(ts // tk):
            vt_ref[hd, c, 0:V_HEAD_DIM, :] = (
                vt[hd * V_HEAD_DIM:(hd + 1) * V_HEAD_DIM, c * tk:(c + 1) * tk].astype(BF16))
            vt_ref[hd, c, V_HEAD_DIM:VT_ROWS, :] = ones_rows
    gbs_ref[...] = _silu(gb)

    zc_ext[POOL_HALO:POOL_HALO + ts, :] = zc
    lane = lax.broadcasted_iota(jnp.int32, (1, LANES), 1)
    pos1 = (t * ts + row + 1).astype(F32)
    pooled = []
    for blk in range(D_POOL // LANES):
        w_small, w_big = POOL_WINDOWS[2 * blk], POOL_WINDOWS[2 * blk + 1]
        cols = slice(blk * LANES, (blk + 1) * LANES)
        acc = zc_ext[:, cols]
        sums = {}
        w = 1
        while w < w_big:
            acc = acc + pltpu.roll(acc, w, axis=0)
            w *= 2
            sums[w] = acc[POOL_HALO:, :]
        first = lane < POOL_GROUP_DIM
        wsum = jnp.where(first, sums[w_small], sums[w_big])
        cnt = jnp.minimum(pos1, jnp.where(first, float(w_small), float(w_big)))
        pooled.append(wsum / cnt - zc[:, cols])
    zc_ext[0:POOL_HALO, :] = zc_ext[ts:ts + POOL_HALO, :]
    pooled = jnp.concatenate(pooled, axis=1).astype(BF16)
    yc_ref[...] = (_dot(pooled, wpool_ref[...]) * pscale_ref[...] * _silu(gc)).astype(BF16)


def _layer_spec(a, layer):
    return pl.BlockSpec((None,) + a.shape[1:], lambda b, t: (layer,) + (0,) * (a.ndim - 1))


def _proj_call(x, packed, layer, tabs, t_att, prev=None):
    B, S, _ = x.shape
    ts = T_PROJ
    nt = S // ts
    row_blk = lambda w: pl.BlockSpec((None, ts, w), lambda b, t: (b, t, 0))
    full = lambda a: _layer_spec(a, layer)
    tab = pl.BlockSpec((ts, LANES), lambda b, t: (t, 0))
    weights = [packed[n] for n in ("norm_g", "w_in", "conv_w", "conv_b", "w_gate", "b_gate", "lam",
                                   "q_norm_g", "w_uq", "kv_norm_g", "w_uk", "w_vt", "w_pool", "pool_scale")]
    out_shape = (
        jax.ShapeDtypeStruct((B, S, D_LRU), BF16),
        jax.ShapeDtypeStruct((B, S, D_MLA), F32),
        jax.ShapeDtypeStruct((B, S, D_POOL), BF16),
        jax.ShapeDtypeStruct((B, S, D_QK), BF16),
        jax.ShapeDtypeStruct((B, S, D_QK), BF16),
        jax.ShapeDtypeStruct((B, MLA_HEADS, S // t_att, VT_ROWS, t_att), BF16),
    )
    out_specs = (
        row_blk(D_LRU), row_blk(D_MLA), row_blk(D_POOL), row_blk(D_QK), row_blk(D_QK),
        pl.BlockSpec((None, MLA_HEADS, ts // t_att, VT_ROWS, t_att), lambda b, t: (b, 0, t, 0, 0)),
    )
    assert len(weights) + len(tabs) == N_PROJ_PARAMS
    param_specs = [full(w) for w in weights] + [tab, tab, tab]
    if prev is None:
        body, lead, lead_specs = _proj_kernel, (x,), [row_blk(D_MODEL)]
    else:
        body = _proj_fused_kernel
        lead = (x, *prev, packed["w_out"])
        lead_specs = [row_blk(D_MODEL), row_blk(D_LRU), row_blk(D_MLA), row_blk(D_POOL),
                      _layer_spec(packed["w_out"], layer - 1)]
        out_shape = (jax.ShapeDtypeStruct((B, S, D_MODEL), F32),) + out_shape
        out_specs = (row_blk(D_MODEL),) + out_specs
    return pl.pallas_call(
        body,
        grid=(B, nt),
        in_specs=lead_specs + param_specs,
        out_specs=out_specs,
        out_shape=out_shape,
        scratch_shapes=[
            pltpu.VMEM((ts + CONV_HALO, D_LRU), F32),
            pltpu.VMEM((ts + POOL_HALO, D_POOL), F32),
            pltpu.VMEM((1, D_LRU), F32),
        ],
        compiler_params=pltpu.CompilerParams(dimension_semantics=("arbitrary", "arbitrary")),
    )(*lead, *weights, *tabs)


def _attn_kernel(q_ref, k_ref, vt_ref, g_ref, o_ref, s_a, s_b):
    T = vt_ref.shape[-1]
    g = pl.program_id(2)
    n_c = T // Q_CHUNK
    chains = [(h, c) for h in range(2) for c in range(n_c)]
    n_chains = len(chains)

    def score_chain(r, j, n, nxt):
        h, c = chains[n]
        q = q_ref[r * T + c * Q_CHUNK:r * T + (c + 1) * Q_CHUNK, h * HEAD_PAD:(h + 1) * HEAD_PAD]
        s = _dot_nt(k_ref[pl.ds(pl.multiple_of(j * T, T), T), h * HEAD_PAD:(h + 1) * HEAD_PAD], q)
        nxt[h, :, c * Q_CHUNK:(c + 1) * Q_CHUNK] = s
        return jnp.max(s, axis=0, keepdims=True)

    def update_chain(j, n, cur, state, cmax, diag):
        h, c = chains[n]
        m, acc = state[n]
        rows = (c + 1) * Q_CHUNK if diag else T

        def load():
            s = cur[h, 0:rows, c * Q_CHUNK:(c + 1) * Q_CHUNK]
            if diag:
                kv_pos = lax.broadcasted_iota(jnp.int32, (rows, Q_CHUNK), 0)
                q_pos = lax.broadcasted_iota(jnp.int32, (rows, Q_CHUNK), 1) + c * Q_CHUNK
                s = jnp.where(kv_pos <= q_pos, s, NEG)
            return s

        tile_max = jnp.max(load(), axis=0, keepdims=True) if diag else cmax[n]
        m_new = jnp.maximum(m, tile_max)
        alpha = jnp.exp2(m - m_new)
        p = jnp.exp2(load() - m_new).astype(BF16)
        return m_new, alpha * acc + _dot(vt_ref[h, j, :, 0:rows], p)

    def stage(r, j, cur, nxt, carry):
        state, cmax = carry
        new_state, new_cmax = [], []
        for n in range(n_chains):
            new_cmax.append(score_chain(r, j + 1, n, nxt))
            new_state.append(update_chain(j, n, cur, state, cmax, False))
        return tuple(new_state), tuple(new_cmax)

    def finish(r, j, cur, carry, nxt):
        state, new_cmax = [], []
        for n in range(n_chains):
            if nxt is not None:
                new_cmax.append(score_chain(r + 1, 0, n, nxt))
            state.append(update_chain(j, n, cur, carry[0], None, True))
        outs = [acc[:V_HEAD_DIM] * (1.0 / acc[V_HEAD_DIM:V_HEAD_DIM + 1]) for _, acc in state]
        out_t = jnp.concatenate(
            [jnp.concatenate(outs[h * n_c:(h + 1) * n_c], axis=1) for h in range(2)], axis=0)
        o_ref[r * T:(r + 1) * T, :] = (out_t.T * g_ref[r * T:(r + 1) * T, :]).astype(BF16)
        return tuple(new_cmax)

    init = tuple((jnp.full((1, Q_CHUNK), NEG, F32), jnp.zeros((VT_ROWS, Q_CHUNK), F32))
                 for _ in chains)
    cmax = tuple(score_chain(0, 0, n, s_a) for n in range(n_chains))
    first, other = s_a, s_b
    for r in range(Q_TILES_PER_STEP):
        n_full = Q_TILES_PER_STEP * g + r
        x, y = first, other

        def pair(j, carry, r=r, x=x, y=y):
            return stage(r, j + 1, y, x, stage(r, j, x, y, carry))

        def quad(u, carry):
            return pair(4 * u + 2, pair(4 * u, carry))

        carry = lax.fori_loop(0, g, quad, (init, cmax))
        if r >= 2:
            carry = pair(n_full - r, carry)
        if r % 2:
            carry = stage(r, n_full - 1, x, y, carry)
            x, y = y, x
        last = r == Q_TILES_PER_STEP - 1
        cmax = finish(r, n_full, x, carry, None if last else y)
        first, other = y, x


def _attn_call(q, k, vt, gbs):
    B, S, _ = q.shape
    T = vt.shape[-1]
    nq = S // T
    rows = Q_TILES_PER_STEP * T
    pair = 2 * HEAD_PAD
    return pl.pallas_call(
        _attn_kernel,
        grid=(B, MLA_HEADS // 2, nq // Q_TILES_PER_STEP),
        in_specs=[
            pl.BlockSpec((None, rows, pair), lambda b, hp, g: (b, g, hp)),
            pl.BlockSpec((None, S, pair), lambda b, hp, g: (b, 0, hp)),
            pl.BlockSpec((None, 2, nq, VT_ROWS, T), lambda b, hp, g: (b, hp, 0, 0, 0)),
            pl.BlockSpec((None, rows, 2 * V_HEAD_DIM), lambda b, hp, g: (b, g, hp)),
        ],
        out_specs=pl.BlockSpec((None, rows, 2 * V_HEAD_DIM), lambda b, hp, g: (b, g, hp)),
        out_shape=jax.ShapeDtypeStruct((B, S, D_MLA), BF16),
        scratch_shapes=[pltpu.VMEM((2, T, T), F32), pltpu.VMEM((2, T, T), F32)],
        compiler_params=pltpu.CompilerParams(
            dimension_semantics=("arbitrary", "arbitrary", "arbitrary")),
    )(q, k, vt, gbs)


def _out_kernel(x_ref, ya_ref, yb_ref, yc_ref, w_ref, fg_ref, o_ref):
    y = jnp.concatenate([ya_ref[...], yb_ref[...], yc_ref[...]], axis=1)
    o_ref[...] = _rms(x_ref[...] + _dot(y, w_ref[...]), fg_ref[...])


def _out_call(x, ya, yb, yc, w_out, layer, final_g):
    B, S, _ = x.shape
    ts = T_OUT
    row_blk = lambda w: pl.BlockSpec((None, ts, w), lambda b, t: (b, t, 0))
    return pl.pallas_call(
        _out_kernel,
        grid=(B, S // ts),
        in_specs=[row_blk(D_MODEL), row_blk(D_LRU), row_blk(D_MLA), row_blk(D_POOL),
                  _layer_spec(w_out, layer), pl.BlockSpec(final_g.shape, lambda b, t: (0, 0))],
        out_specs=row_blk(D_MODEL),
        out_shape=jax.ShapeDtypeStruct((B, S, D_MODEL), F32),
        compiler_params=pltpu.CompilerParams(dimension_semantics=("arbitrary", "arbitrary")),
    )(x, ya, yb, yc, w_out, final_g)


def _block_diag(w):
    dp, g, n, _ = w.shape
    eye = jnp.eye(g, dtype=w.dtype)
    return jnp.einsum("dgij,gh->dgihj", w, eye).reshape(dp, g * n, g * n)


def _pack_weights(norm_g, w_in, conv_w, conv_b, w_rg, b_rg, w_ig, b_ig, lru_lambda,
                  q_norm_g, w_uq, kv_norm_g, w_ukv, w_pool, pool_scale, w_out):
    kr_lo = int(np.sum(IN_SIZES[:4]))
    kr_hi = kr_lo + QK_ROPE_DIM
    zeros = lambda n: jnp.zeros(w_in.shape[:2] + (n,), BF16)
    w_in_b = w_in.astype(BF16)
    w_in_p = jnp.concatenate([w_in_b[:, :, :kr_lo], zeros(ROPE_LO), w_in_b[:, :, kr_lo:kr_hi],
                              zeros(HEAD_PAD - ROPE_LO - QK_ROPE_DIM), w_in_b[:, :, kr_hi:]], axis=-1)
    assert w_in_p.shape[-1] == D_IN_PAD
    w_gate = jnp.concatenate([_block_diag(w_rg), _block_diag(w_ig)], axis=-1).astype(BF16)
    b_gate = jnp.concatenate([b_rg, b_ig], axis=-1)[:, None, :]
    uq = w_uq.reshape(DEPTH, Q_LORA_RANK, MLA_HEADS, QK_NOPE_DIM + QK_ROPE_DIM)
    uq = jnp.pad(uq, ((0, 0), (0, 0), (0, 0), (0, HEAD_PAD - QK_NOPE_DIM - QK_ROPE_DIM)))
    ukv = w_ukv.reshape(DEPTH, KV_LORA_RANK, MLA_HEADS, QK_NOPE_DIM + V_HEAD_DIM)
    uk = jnp.pad(ukv[..., :QK_NOPE_DIM], ((0, 0), (0, 0), (0, 0), (0, HEAD_PAD - QK_NOPE_DIM)))
    uv = ukv[..., QK_NOPE_DIM:].reshape(DEPTH, KV_LORA_RANK, D_MLA)
    return {
        "norm_g": norm_g[:, None, :],
        "w_in": w_in_p,
        "conv_w": conv_w,
        "conv_b": conv_b[:, None, :],
        "w_gate": w_gate,
        "b_gate": b_gate,
        "lam": lru_lambda[:, None, :],
        "q_norm_g": q_norm_g[:, None, :],
        "w_uq": uq.reshape(DEPTH, Q_LORA_RANK, D_QK).astype(BF16),
        "kv_norm_g": kv_norm_g[:, None, :],
        "w_uk": uk.reshape(DEPTH, KV_LORA_RANK, D_QK).astype(BF16),
        "w_vt": jnp.swapaxes(uv, 1, 2).astype(BF16),
        "w_pool": _block_diag(w_pool).astype(BF16),
        "pool_scale": pool_scale[:, None, :],
        "w_out": w_out.astype(BF16),
    }


def _rope_tables(seq_len):
    pos = jnp.arange(seq_len, dtype=F32)
    inv_freq = ROPE_BASE ** (-jnp.arange(0, QK_ROPE_DIM, 2, dtype=F32) / QK_ROPE_DIM)
    ang = pos[:, None] * inv_freq[None, :]
    cos, sin = jnp.cos(ang), jnp.sin(ang)
    zeros = jnp.zeros_like(sin)
    tail = jnp.zeros((seq_len, HEAD_PAD - ROPE_LO - QK_ROPE_DIM), F32)
    ones = jnp.ones((seq_len, ROPE_LO), F32)
    lead = jnp.zeros((seq_len, ROPE_LO), F32)
    cos_t = jnp.concatenate([ones, cos, cos, tail], axis=1)
    sin_hi = jnp.concatenate([lead, -sin, zeros, tail], axis=1)
    sin_lo = jnp.concatenate([lead, zeros, sin, tail], axis=1)
    return cos_t, sin_hi, sin_lo


def kernel(x, norm_g, w_in, conv_w, conv_b, w_rg, b_rg, w_ig, b_ig, lru_lambda, q_norm_g, w_uq,
           kv_norm_g, w_ukv, w_pool, pool_scale, w_out, final_norm_g):
    B, S, D = x.shape
    assert D == D_MODEL and S % T_PROJ == 0 and S % T_ATT == 0 and T_PROJ % T_ATT == 0
    packed = _pack_weights(norm_g, w_in, conv_w, conv_b, w_rg, b_rg, w_ig, b_ig, lru_lambda,
                           q_norm_g, w_uq, kv_norm_g, w_ukv, w_pool, pool_scale, w_out)
    tabs = _rope_tables(S)
    final_g = final_norm_g[None, :]
    mixed = None
    for l in range(DEPTH):
        if mixed is None:
            ya, gbs, yc, q, k, vt = _proj_call(x, packed, l, tabs, T_ATT)
        else:
            x, ya, gbs, yc, q, k, vt = _proj_call(x, packed, l, tabs, T_ATT, prev=mixed)
        mixed = (ya, _attn_call(q, k, vt, gbs), yc)
    return _out_call(x, *mixed, packed["w_out"], DEPTH - 1, final_g)
```
